```python
import functools
import jax, jax.numpy as jnp
from jax import lax
import numpy as np

D_MODEL = 1024
BATCH = 8
SEQ = 8192
DEPTH = 4

MEM_LEN = 256
Q_BLOCK = 128
MAX_GROUPS = 8
N_BRANCHES = 4
BRANCH_WIDTH = 256
MLA_HEADS = 4
MLA_NOPE = 64
MLA_ROPE = 32
MLA_V = 64
MLA_Q_RANK = 384
MLA_KV_RANK = 256
ROPE_THETA = 10000.0
SB_HEADS = 4
SB_HEAD_DIM = 64
FOX_HEADS = 4
FOX_HEAD_DIM = 64
MEM_HEADS = 4
MEM_HEAD_DIM = 64
MERGE_RANK = 128
RMS_EPS = 1e-6
LN_EPS = 1e-5
DEEPNORM_ALPHA = (2 * DEPTH) ** 0.25
DEEPNORM_BETA = (8 * DEPTH) ** -0.25

IN_SPLITS = (
    MLA_Q_RANK,
    MLA_KV_RANK,
    MLA_ROPE,
    3 * SB_HEADS * SB_HEAD_DIM,
    3 * FOX_HEADS * FOX_HEAD_DIM,
    FOX_HEADS,
    MEM_HEADS * MEM_HEAD_DIM,
    N_BRANCHES * BRANCH_WIDTH,
    MERGE_RANK,
)
IN_WIDTH = sum(IN_SPLITS)

kernel_name = "hybrid_mla_stickbreak_fox_memory_deepnorm"


def _split_last(t, sizes):
    out, start = [], 0
    for n in sizes:
        out.append(t[..., start:start + n])
        start += n
    return out


def _rms_norm(x, g):
    xf = x.astype(jnp.float32)
    y = xf * lax.rsqrt(jnp.mean(xf * xf, axis=-1, keepdims=True) + RMS_EPS)
    return (y * g.astype(jnp.float32)).astype(x.dtype)


def _layer_norm(x, g, b):
    xf = x.astype(jnp.float32)
    mu = jnp.mean(xf, axis=-1, keepdims=True)
    var = jnp.mean(jnp.square(xf - mu), axis=-1, keepdims=True)
    y = (xf - mu) * lax.rsqrt(var + LN_EPS)
    return (y * g.astype(jnp.float32) + b.astype(jnp.float32)).astype(x.dtype)


def _rope(x, cos, sin):
    x1, x2 = jnp.split(x, 2, axis=-1)
    return jnp.concatenate([x1 * cos - x2 * sin, x1 * sin + x2 * cos], axis=-1)


def _causal_sweep(block_fn, s):
    n_blocks = s // Q_BLOCK
    n_groups = min(MAX_GROUPS, n_blocks)
    bounds = [(g * n_blocks) // n_groups for g in range(n_groups + 1)]
    outs = []
    for g in range(n_groups):
        b0, b1 = bounds[g], bounds[g + 1]
        o = lax.map(functools.partial(block_fn, kend=b1 * Q_BLOCK), jnp.arange(b0, b1))
        nb, b, qn, h, dv = o.shape
        outs.append(o.transpose(1, 0, 2, 3, 4).reshape(b, nb * qn, h, dv))
    return jnp.concatenate(outs, axis=1)


def _causal_softmax_attention(q, k, v, log_decay=None):
    b, s, h, d = q.shape
    q = q * (d ** -0.5)
    c_t = None if log_decay is None else log_decay.transpose(0, 2, 1)

    def block(i, kend):
        start = i * Q_BLOCK
        qb = lax.dynamic_slice_in_dim(q, start, Q_BLOCK, axis=1)
        kb, vb = k[:, :kend], v[:, :kend]
        logits = jnp.einsum('bqhd,bkhd->bhqk', qb, kb, preferred_element_type=jnp.float32)
        if c_t is not None:
            cq = lax.dynamic_slice_in_dim(c_t, start, Q_BLOCK, axis=2)
            logits = logits + (cq[..., :, None] - c_t[..., None, :kend])
        causal = jnp.arange(kend)[None, :] <= (start + jnp.arange(Q_BLOCK))[:, None]
        logits = jnp.where(causal, logits, -jnp.inf)
        m = jnp.max(logits, axis=-1, keepdims=True)
        p = jnp.exp(logits - m)
        denom = jnp.sum(p, axis=-1).transpose(0, 2, 1)[..., None]
        o = jnp.einsum('bhqk,bkhd->bqhd', p.astype(v.dtype), vb, preferred_element_type=jnp.float32)
        return (o / denom).astype(v.dtype)

    return _causal_sweep(block, s)


def _stick_breaking_attention(q, k, v):
    b, s, h, d = q.shape
    q = q * (d ** -0.5)
    tri = jnp.tril(jnp.ones((Q_BLOCK, Q_BLOCK), jnp.float32))

    def block(i, kend):
        start = i * Q_BLOCK
        nk = kend // Q_BLOCK
        qb = lax.dynamic_slice_in_dim(q, start, Q_BLOCK, axis=1)
        kb, vb = k[:, :kend], v[:, :kend]
        z = jnp.einsum('bqhd,bkhd->bhqk', qb, kb, preferred_element_type=jnp.float32)
        strict = jnp.arange(kend)[None, :] < (start + jnp.arange(Q_BLOCK))[:, None]
        log_keep = jnp.where(strict, jax.nn.log_sigmoid(-z), 0.0)
        local = jnp.einsum('bhqnc,cr->bhqnr', log_keep.reshape(b, h, Q_BLOCK, nk, Q_BLOCK), tri)
        tot = local[..., 0]
        cross = lax.cumsum(tot, axis=3, reverse=True) - tot
        incl = (local + cross[..., None]).reshape(b, h, Q_BLOCK, kend)
        a = jnp.where(strict, jnp.exp(z + incl), 0.0).astype(v.dtype)
        return jnp.einsum('bhqk,bkhd->bqhd', a, vb)

    return _causal_sweep(block, s)


def _memory_attention(q, k, v):
    logits = jnp.einsum('bshd,bmhd->bhsm', q * (q.shape[-1] ** -0.5), k,
                        preferred_element_type=jnp.float32)
    p = jax.nn.softmax(logits, axis=-1).astype(v.dtype)
    return jnp.einsum('bhsm,bmhd->bshd', p, v)


def _layer(x, mem, cos, sin, w_in, q_norm, w_qb, kv_norm, w_kvb, fox_bias,
           w_mem_kv, w_merge_up, w_branch, w_out, ln_g, ln_b):
    b, s, _ = x.shape
    h = x @ w_in
    (c_q, c_kv, k_rope, sb_qkv, fox_qkv, fox_f, mem_q, gate_z, merge_r) = _split_last(h, IN_SPLITS)

    q = (_rms_norm(c_q, q_norm) @ w_qb).reshape(b, s, MLA_HEADS, MLA_NOPE + MLA_ROPE)
    q_nope, q_pe = q[..., :MLA_NOPE], q[..., MLA_NOPE:]
    q_pe = _rope(q_pe, cos[:, :, None, :], sin[:, :, None, :])
    kv = (_rms_norm(c_kv, kv_norm) @ w_kvb).reshape(b, s, MLA_HEADS, MLA_NOPE + MLA_V)
    k_nope, v_mla = kv[..., :MLA_NOPE], kv[..., MLA_NOPE:]
    k_pe = _rope(k_rope[:, :, None, :], cos[:, :, None, :], sin[:, :, None, :])
    k_pe = jnp.broadcast_to(k_pe, (b, s, MLA_HEADS, MLA_ROPE))
    y_mla = _causal_softmax_attention(jnp.concatenate([q_nope, q_pe], -1),
                                      jnp.concatenate([k_nope, k_pe], -1), v_mla)

    sq, sk, sv = [t.reshape(b, s, SB_HEADS, SB_HEAD_DIM) for t in jnp.split(sb_qkv, 3, axis=-1)]
    y_sb = _stick_breaking_attention(sq, sk, sv)

    fq, fk, fv = [t.reshape(b, s, FOX_HEADS, FOX_HEAD_DIM) for t in jnp.split(fox_qkv, 3, axis=-1)]
    log_f = jax.nn.log_sigmoid(fox_f.astype(jnp.float32) + fox_bias.astype(jnp.float32))
    c = jnp.cumsum(log_f, axis=1)
    y_fox = _causal_softmax_attention(fq, fk, fv, log_decay=c)

    mk, mv = jnp.split(mem @ w_mem_kv, 2, axis=-1)
    mk = mk.reshape(b, MEM_LEN, MEM_HEADS, MEM_HEAD_DIM)
    mv = mv.reshape(b, MEM_LEN, MEM_HEADS, MEM_HEAD_DIM)
    y_mem = _memory_attention(mem_q.reshape(b, s, MEM_HEADS, MEM_HEAD_DIM), mk, mv)

    branches = (y_mla, y_sb, y_fox, y_mem)
    gate_z = gate_z.reshape(b, s, N_BRANCHES, BRANCH_WIDTH)
    merge_g = jax.nn.sigmoid(merge_r @ w_merge_up).reshape(b, s, N_BRANCHES, D_MODEL)
    merged = jnp.zeros_like(x)
    for n in range(N_BRANCHES):
        yb = branches[n].reshape(b, s, BRANCH_WIDTH) * jax.nn.silu(gate_z[:, :, n])
        merged = merged + merge_g[:, :, n] * (yb @ w_branch[n])
    out = merged @ w_out

    return _layer_norm(DEEPNORM_ALPHA * x + out, ln_g, ln_b)


def _fwd_setup_inputs(seed: int = 0) -> dict:
    key = jax.random.key(seed)
    ks = jax.random.split(key, 16)
    f32 = jnp.float32

    def nrm(k, shape, scale):
        return jax.random.normal(k, shape, f32) * scale

    x = nrm(ks[0], (BATCH, SEQ, D_MODEL), 1.0)
    mem = nrm(ks[1], (BATCH, MEM_LEN, D_MODEL), 1.0)
    offsets = jax.random.randint(ks[2], (BATCH, 1), 0, 4096, dtype=jnp.int32)
    positions = (jnp.arange(SEQ, dtype=jnp.int32)[None, :] + offsets).astype(jnp.int32)

    w_in = nrm(ks[3], (DEPTH, D_MODEL, IN_WIDTH), D_MODEL ** -0.5)
    mla_q_norm = 1.0 + nrm(ks[4], (DEPTH, MLA_Q_RANK), 0.02)
    mla_w_qb = nrm(ks[5], (DEPTH, MLA_Q_RANK, MLA_HEADS * (MLA_NOPE + MLA_ROPE)), MLA_Q_RANK ** -0.5)
    mla_kv_norm = 1.0 + nrm(ks[6], (DEPTH, MLA_KV_RANK), 0.02)
    mla_w_kvb = nrm(ks[7], (DEPTH, MLA_KV_RANK, MLA_HEADS * (MLA_NOPE + MLA_V)), MLA_KV_RANK ** -0.5)
    fox_forget_bias = jax.random.uniform(ks[8], (DEPTH, FOX_HEADS), f32, 1.0, 4.0)
    w_mem_kv = nrm(ks[9], (DEPTH, D_MODEL, 2 * MEM_HEADS * MEM_HEAD_DIM), D_MODEL ** -0.5)
    w_merge_up = nrm(ks[10], (DEPTH, MERGE_RANK, N_BRANCHES * D_MODEL), MERGE_RANK ** -0.5)
    w_branch = nrm(ks[11], (DEPTH, N_BRANCHES, BRANCH_WIDTH, D_MODEL), BRANCH_WIDTH ** -0.5 * DEEPNORM_BETA)
    w_out = nrm(ks[12], (DEPTH, D_MODEL, D_MODEL), D_MODEL ** -0.5 * DEEPNORM_BETA)
    ln_gain = 1.0 + nrm(ks[13], (DEPTH, D_MODEL), 0.02)
    ln_bias = nrm(ks[14], (DEPTH, D_MODEL), 0.02)
    return {"x": x, "mem": mem, "positions": positions, "w_in": w_in,
            "mla_q_norm": mla_q_norm, "mla_w_qb": mla_w_qb,
            "mla_kv_norm": mla_kv_norm, "mla_w_kvb": mla_w_kvb,
            "fox_forget_bias": fox_forget_bias, "w_mem_kv": w_mem_kv,
            "w_merge_up": w_merge_up, "w_branch": w_branch, "w_out": w_out,
            "ln_gain": ln_gain, "ln_bias": ln_bias}


def _fwd_reference(x, mem, positions, w_in, mla_q_norm, mla_w_qb, mla_kv_norm, mla_w_kvb,
              fox_forget_bias, w_mem_kv, w_merge_up, w_branch, w_out, ln_gain, ln_bias):
    inv_freq = ROPE_THETA ** (-jnp.arange(0, MLA_ROPE, 2, dtype=jnp.float32) / MLA_ROPE)
    ang = positions.astype(jnp.float32)[..., None] * inv_freq
    cos = jnp.cos(ang).astype(x.dtype)
    sin = jnp.sin(ang).astype(x.dtype)
    for l in range(DEPTH):
        x = _layer(x, mem, cos, sin, w_in[l], mla_q_norm[l], mla_w_qb[l],
                   mla_kv_norm[l], mla_w_kvb[l], fox_forget_bias[l], w_mem_kv[l],
                   w_merge_up[l], w_branch[l], w_out[l], ln_gain[l], ln_bias[l])
    return x


import jax as _jax
import jax.numpy as _jnp

TWIN_FORMAT = 'train_step'
FWD_PARAMS = ['x', 'mem', 'positions', 'w_in', 'mla_q_norm', 'mla_w_qb', 'mla_kv_norm', 'mla_w_kvb', 'fox_forget_bias', 'w_mem_kv', 'w_merge_up', 'w_branch', 'w_out', 'ln_gain', 'ln_bias']
TWIN_WEIGHTS = ['w_in', 'mla_q_norm', 'mla_w_qb', 'mla_kv_norm', 'mla_w_kvb', 'fox_forget_bias', 'w_mem_kv', 'w_merge_up', 'w_branch', 'w_out', 'ln_gain', 'ln_bias']
TWIN_DIFF_INPUT = 'x'
TWIN_INPUTS = ['x', 'mem', 'positions', 'w_in', 'mla_q_norm', 'mla_w_qb', 'mla_kv_norm', 'mla_w_kvb', 'fox_forget_bias', 'w_mem_kv', 'w_merge_up', 'w_branch', 'w_out', 'ln_gain', 'ln_bias', 'loss_target', 'm_w_in', 'm_mla_q_norm', 'm_mla_w_qb', 'm_mla_kv_norm', 'm_mla_w_kvb', 'm_fox_forget_bias', 'm_w_mem_kv', 'm_w_merge_up', 'm_w_branch', 'm_w_out', 'm_ln_gain', 'm_ln_bias', 'v_w_in', 'v_mla_q_norm', 'v_mla_w_qb', 'v_mla_kv_norm', 'v_mla_w_kvb', 'v_fox_forget_bias', 'v_w_mem_kv', 'v_w_merge_up', 'v_w_branch', 'v_w_out', 'v_ln_gain', 'v_ln_bias']
TWIN_OUTPUTS = ['loss', 'grad_x', 'grad_w_in', 'grad_mla_q_norm', 'grad_mla_w_qb', 'grad_mla_kv_norm', 'grad_mla_w_kvb', 'grad_fox_forget_bias', 'grad_w_mem_kv', 'grad_w_merge_up', 'grad_w_branch', 'grad_w_out', 'grad_ln_gain', 'grad_ln_bias', 'delta_w_in', 'delta_mla_q_norm', 'delta_mla_w_qb', 'delta_mla_kv_norm', 'delta_mla_w_kvb', 'delta_fox_forget_bias', 'delta_w_mem_kv', 'delta_w_merge_up', 'delta_w_branch', 'delta_w_out', 'delta_ln_gain', 'delta_ln_bias', 'new_m_w_in', 'new_m_mla_q_norm', 'new_m_mla_w_qb', 'new_m_mla_kv_norm', 'new_m_mla_w_kvb', 'new_m_fox_forget_bias', 'new_m_w_mem_kv', 'new_m_w_merge_up', 'new_m_w_branch', 'new_m_w_out', 'new_m_ln_gain', 'new_m_ln_bias', 'new_v_w_in', 'new_v_mla_q_norm', 'new_v_mla_w_qb', 'new_v_mla_kv_norm', 'new_v_mla_w_kvb', 'new_v_fox_forget_bias', 'new_v_w_mem_kv', 'new_v_w_merge_up', 'new_v_w_branch', 'new_v_w_out', 'new_v_ln_gain', 'new_v_ln_bias']
TWIN_LEAF_KINDS = {'loss': 'loss', 'grad_x': 'grad_x', 'grad_w_in': 'grad_w', 'grad_mla_q_norm': 'grad_w', 'grad_mla_w_qb': 'grad_w', 'grad_mla_kv_norm': 'grad_w', 'grad_mla_w_kvb': 'grad_w', 'grad_fox_forget_bias': 'grad_w', 'grad_w_mem_kv': 'grad_w', 'grad_w_merge_up': 'grad_w', 'grad_w_branch': 'grad_w', 'grad_w_out': 'grad_w', 'grad_ln_gain': 'grad_w', 'grad_ln_bias': 'grad_w', 'delta_w_in': 'delta_w', 'delta_mla_q_norm': 'delta_w', 'delta_mla_w_qb': 'delta_w', 'delta_mla_kv_norm': 'delta_w', 'delta_mla_w_kvb': 'delta_w', 'delta_fox_forget_bias': 'delta_w', 'delta_w_mem_kv': 'delta_w', 'delta_w_merge_up': 'delta_w', 'delta_w_branch': 'delta_w', 'delta_w_out': 'delta_w', 'delta_ln_gain': 'delta_w', 'delta_ln_bias': 'delta_w', 'new_m_w_in': 'new_m', 'new_m_mla_q_norm': 'new_m', 'new_m_mla_w_qb': 'new_m', 'new_m_mla_kv_norm': 'new_m', 'new_m_mla_w_kvb': 'new_m', 'new_m_fox_forget_bias': 'new_m', 'new_m_w_mem_kv': 'new_m', 'new_m_w_merge_up': 'new_m', 'new_m_w_branch': 'new_m', 'new_m_w_out': 'new_m', 'new_m_ln_gain': 'new_m', 'new_m_ln_bias': 'new_m', 'new_v_w_in': 'new_v', 'new_v_mla_q_norm': 'new_v', 'new_v_mla_w_qb': 'new_v', 'new_v_mla_kv_norm': 'new_v', 'new_v_mla_w_kvb': 'new_v', 'new_v_fox_forget_bias': 'new_v', 'new_v_w_mem_kv': 'new_v', 'new_v_w_merge_up': 'new_v', 'new_v_w_branch': 'new_v', 'new_v_w_out': 'new_v', 'new_v_ln_gain': 'new_v', 'new_v_ln_bias': 'new_v'}


def _forward(args):
    return _fwd_reference(*[args[k] for k in FWD_PARAMS])


def _output_shape():
    def fwd():
        inp = _fwd_setup_inputs(0)
        return _fwd_reference(*[inp[k] for k in FWD_PARAMS])
    out = _jax.eval_shape(fwd)
    return out.shape, out.dtype

N_MICROBATCH = 1
ADAM_LR = 0.001
ADAM_B1 = 0.9
ADAM_B2 = 0.999
ADAM_EPS = 1e-08
ADAM_WD = 0.01
ADAM_STEP = 10
PER_EXAMPLE_BATCH_AXIS = {'x': 0, 'mem': 0, 'positions': 0, 'loss_target': 0}
SHARED_INPUTS = []
_WEIGHT_DTYPES = {'w_in': _jnp.float32, 'mla_q_norm': _jnp.float32, 'mla_w_qb': _jnp.float32, 'mla_kv_norm': _jnp.float32, 'mla_w_kvb': _jnp.float32, 'fox_forget_bias': _jnp.float32, 'w_mem_kv': _jnp.float32, 'w_merge_up': _jnp.float32, 'w_branch': _jnp.float32, 'w_out': _jnp.float32, 'ln_gain': _jnp.float32, 'ln_bias': _jnp.float32}
MOMENT_SCALE = {'w_in': 6.152721e-03, 'mla_q_norm': 2.554202e-03, 'mla_w_qb': 2.531344e-03, 'mla_kv_norm': 4.698565e-03, 'mla_w_kvb': 3.228405e-03, 'fox_forget_bias': 5.256064e-02, 'w_mem_kv': 1.825874e-03, 'w_merge_up': 1.326353e-03, 'w_branch': 8.079638e-03, 'w_out': 1.615361e-02, 'ln_gain': 3.210580e+01, 'ln_bias': 1.409729e+00}


def _to_microbatches(a, axis):
    t = _jnp.moveaxis(a, axis, 0)
    t = t.reshape((N_MICROBATCH, t.shape[0] // N_MICROBATCH) + t.shape[1:])
    return _jnp.moveaxis(t, 1, axis + 1)


def setup_inputs(seed: int = 0) -> dict:
    inp = _fwd_setup_inputs(seed)
    key = _jax.random.fold_in(_jax.random.key(seed), 7919)
    shape, _ = _output_shape()
    out = dict(inp)
    out["loss_target"] = _jax.random.normal(_jax.random.fold_in(key, 0), shape, _jnp.float32)
    for i, name in enumerate(TWIN_WEIGHTS):
        w = inp[name].astype(_jnp.float32)
        if MOMENT_SCALE is None:
            s = _jnp.sqrt(_jnp.mean(_jnp.square(w)) + 1e-30)
        else:
            s = MOMENT_SCALE[name]
        km, kv = _jax.random.split(_jax.random.fold_in(key, i + 1))
        out[name] = w
        out["m_" + name] = s * _jax.random.normal(km, w.shape, _jnp.float32)
        out["v_" + name] = (s * s) * _jax.random.uniform(kv, w.shape, _jnp.float32, 0.5, 1.5)
    if N_MICROBATCH > 1:
        for name, axis in PER_EXAMPLE_BATCH_AXIS.items():
            out[name] = _to_microbatches(out[name], axis)
    return {'x': out['x'], 'mem': out['mem'], 'positions': out['positions'], 'w_in': out['w_in'], 'mla_q_norm': out['mla_q_norm'], 'mla_w_qb': out['mla_w_qb'], 'mla_kv_norm': out['mla_kv_norm'], 'mla_w_kvb': out['mla_w_kvb'], 'fox_forget_bias': out['fox_forget_bias'], 'w_mem_kv': out['w_mem_kv'], 'w_merge_up': out['w_merge_up'], 'w_branch': out['w_branch'], 'w_out': out['w_out'], 'ln_gain': out['ln_gain'], 'ln_bias': out['ln_bias'], 'loss_target': out['loss_target'], 'm_w_in': out['m_w_in'], 'm_mla_q_norm': out['m_mla_q_norm'], 'm_mla_w_qb': out['m_mla_w_qb'], 'm_mla_kv_norm': out['m_mla_kv_norm'], 'm_mla_w_kvb': out['m_mla_w_kvb'], 'm_fox_forget_bias': out['m_fox_forget_bias'], 'm_w_mem_kv': out['m_w_mem_kv'], 'm_w_merge_up': out['m_w_merge_up'], 'm_w_branch': out['m_w_branch'], 'm_w_out': out['m_w_out'], 'm_ln_gain': out['m_ln_gain'], 'm_ln_bias': out['m_ln_bias'], 'v_w_in': out['v_w_in'], 'v_mla_q_norm': out['v_mla_q_norm'], 'v_mla_w_qb': out['v_mla_w_qb'], 'v_mla_kv_norm': out['v_mla_kv_norm'], 'v_mla_w_kvb': out['v_mla_w_kvb'], 'v_fox_forget_bias': out['v_fox_forget_bias'], 'v_w_mem_kv': out['v_w_mem_kv'], 'v_w_merge_up': out['v_w_merge_up'], 'v_w_branch': out['v_w_branch'], 'v_w_out': out['v_w_out'], 'v_ln_gain': out['v_ln_gain'], 'v_ln_bias': out['v_ln_bias']}


def _loss(weights, diff, rest, loss_target):
    with _jax.named_scope("forward"):
        args = {**rest, TWIN_DIFF_INPUT: diff, **{k: w.astype(_WEIGHT_DTYPES[k]) for k, w in weights.items()}}
        y = _forward(args)
    with _jax.named_scope("loss_head"):
        err = _jnp.square(y.astype(_jnp.float32) - loss_target)
        return 0.5 * _jnp.sum(_jnp.mean(err, axis=-1)) if err.ndim else 0.5 * err


def _adamw(w, g, m, v):
    m = ADAM_B1 * m + (1.0 - ADAM_B1) * g
    v = ADAM_B2 * v + (1.0 - ADAM_B2) * _jnp.square(g)
    m_hat = m / (1.0 - ADAM_B1 ** ADAM_STEP)
    v_hat = v / (1.0 - ADAM_B2 ** ADAM_STEP)
    delta = -ADAM_LR * (m_hat / (_jnp.sqrt(v_hat) + ADAM_EPS) + ADAM_WD * w)
    return delta, m, v


def reference(x, mem, positions, w_in, mla_q_norm, mla_w_qb, mla_kv_norm, mla_w_kvb, fox_forget_bias, w_mem_kv, w_merge_up, w_branch, w_out, ln_gain, ln_bias, loss_target, m_w_in, m_mla_q_norm, m_mla_w_qb, m_mla_kv_norm, m_mla_w_kvb, m_fox_forget_bias, m_w_mem_kv, m_w_merge_up, m_w_branch, m_w_out, m_ln_gain, m_ln_bias, v_w_in, v_mla_q_norm, v_mla_w_qb, v_mla_kv_norm, v_mla_w_kvb, v_fox_forget_bias, v_w_mem_kv, v_w_merge_up, v_w_branch, v_w_out, v_ln_gain, v_ln_bias):
    given = dict(x=x, mem=mem, positions=positions, w_in=w_in, mla_q_norm=mla_q_norm, mla_w_qb=mla_w_qb, mla_kv_norm=mla_kv_norm, mla_w_kvb=mla_w_kvb, fox_forget_bias=fox_forget_bias, w_mem_kv=w_mem_kv, w_merge_up=w_merge_up, w_branch=w_branch, w_out=w_out, ln_gain=ln_gain, ln_bias=ln_bias, loss_target=loss_target, m_w_in=m_w_in, m_mla_q_norm=m_mla_q_norm, m_mla_w_qb=m_mla_w_qb, m_mla_kv_norm=m_mla_kv_norm, m_mla_w_kvb=m_mla_w_kvb, m_fox_forget_bias=m_fox_forget_bias, m_w_mem_kv=m_w_mem_kv, m_w_merge_up=m_w_merge_up, m_w_branch=m_w_branch, m_w_out=m_w_out, m_ln_gain=m_ln_gain, m_ln_bias=m_ln_bias, v_w_in=v_w_in, v_mla_q_norm=v_mla_q_norm, v_mla_w_qb=v_mla_w_qb, v_mla_kv_norm=v_mla_kv_norm, v_mla_w_kvb=v_mla_w_kvb, v_fox_forget_bias=v_fox_forget_bias, v_w_mem_kv=v_w_mem_kv, v_w_merge_up=v_w_merge_up, v_w_branch=v_w_branch, v_w_out=v_w_out, v_ln_gain=v_ln_gain, v_ln_bias=v_ln_bias)
    weights = {n: given[n] for n in TWIN_WEIGHTS}
    shared = {n: given[n] for n in SHARED_INPUTS}
    per_example = {n: given[n] for n in ['x', 'mem', 'positions']}
    grad_fn = _jax.value_and_grad(_loss, argnums=(0, 1))

    def one_microbatch(ex, loss_target):
        ex = dict(ex)
        diff = ex.pop(TWIN_DIFF_INPUT)
        return grad_fn(weights, diff, {**shared, **ex}, loss_target)

    if N_MICROBATCH == 1:
        loss, (grad_w, grad_x) = one_microbatch(per_example, given["loss_target"])
    else:
        def body(carry, xs):
            loss_sum, grad_sum = carry
            l_k, (gw_k, gx_k) = one_microbatch(xs[0], xs[1])
            with _jax.named_scope("update"):
                return (loss_sum + l_k, _jax.tree.map(_jnp.add, grad_sum, gw_k)), gx_k

        init = (_jnp.zeros((), _jnp.float32), _jax.tree.map(_jnp.zeros_like, weights))
        (loss, grad_w), grad_x = _jax.lax.scan(body, init, (per_example, given["loss_target"]))
    with _jax.named_scope("update"):
        delta_w, new_m, new_v = {}, {}, {}
        for n in TWIN_WEIGHTS:
            delta_w[n], new_m[n], new_v[n] = _adamw(weights[n], grad_w[n], given["m_" + n], given["v_" + n])
    return (loss, grad_x, *[grad_w[n] for n in TWIN_WEIGHTS], *[delta_w[n] for n in TWIN_WEIGHTS],
            *[new_m[n] for n in TWIN_WEIGHTS], *[new_v[n] for n in TWIN_WEIGHTS])
```

```python
import functools

import numpy as np
import jax
import jax.numpy as jnp
from jax import lax
from jax.experimental import pallas as pl
from jax.experimental.pallas import tpu as pltpu

F32 = jnp.float32
MXU_DTYPE = jnp.bfloat16
WIRE_DTYPE = jnp.bfloat16

N_LAYERS = 4
N_BRANCHES = 4
BRANCH_WIDTH = 256
N_HEADS = 4
HEAD_DIM = 64
MLA_NOPE = 64
MLA_ROPE = 32
MLA_Q_RANK = 384
MLA_KV_RANK = 256
ROPE_THETA = 10000.0
MERGE_RANK = 128
RMS_EPS = 1e-6
LN_EPS = 1e-5
DEEPNORM_ALPHA = (2 * N_LAYERS) ** 0.25
IN_SPLITS = (MLA_Q_RANK, MLA_KV_RANK, MLA_ROPE, 3 * N_HEADS * HEAD_DIM, 3 * N_HEADS * HEAD_DIM,
             N_HEADS, N_HEADS * HEAD_DIM, N_BRANCHES * BRANCH_WIDTH, MERGE_RANK)
IN_WIDTH = sum(IN_SPLITS)
ADAM_LR, ADAM_B1, ADAM_B2, ADAM_EPS, ADAM_WD, ADAM_STEP = 0.001, 0.9, 0.999, 1e-08, 0.01, 10

LANES = 128
SUBLANES = 8
VMEM_BYTES_V7X = 64 * 1024 * 1024
VMEM_LIMIT_CAP = VMEM_BYTES_V7X - 8 * 1024 * 1024
MM_VMEM_BUDGET = 44 * 1024 * 1024
ROW_TILE_BYTES = 4 * 1024 * 1024
ATTN_TILE = 512
SUFFIX_BLOCK = 256
N_DEV = 8
IN_WIDTH_PAD = -(-IN_WIDTH // LANES) * LANES
MLA_QK_PAD = 128


def _cparams(semantics, vmem_estimate):
    limit = int(min(max(2 * vmem_estimate, 32 * 1024 * 1024), VMEM_LIMIT_CAP))
    return pltpu.CompilerParams(dimension_semantics=semantics, vmem_limit_bytes=limit)


def _dot(a, b, ca=1, cb=0):
    return lax.dot_general(a, b, (((ca,), (cb,)), ((), ())), preferred_element_type=F32)


def _div_tile(n, cap, align):
    t = (cap // align) * align
    while t >= align:
        if n % t == 0:
            return t
        t -= align
    return n


def _mm_call(a, b, ta, tb, out_dtype, name):
    k, m = a.shape if ta else a.shape[::-1]
    n, kb = b.shape if tb else b.shape[::-1]
    assert k == kb, (a.shape, b.shape, ta, tb)
    ab, bb, ob = a.dtype.itemsize, b.dtype.itemsize, jnp.dtype(out_dtype).itemsize
    tk = k if k <= 4096 else 512
    tn = n if n <= 4096 else _div_tile(n, 2048, LANES)
    cands = [t for t in (1024, 512, 256, 128) if t <= m and m % t == 0] or [m]
    for tm in cands:
        est = 2 * (tm * tk * ab + tk * tn * bb + tm * tn * ob) + (tm * tn * 4 if k > tk else 0)
        if est <= MM_VMEM_BUDGET:
            break
    nk = k // tk

    def body(a_ref, b_ref, o_ref, *scratch):
        p = _dot(a_ref[...].astype(MXU_DTYPE), b_ref[...].astype(MXU_DTYPE), 0 if ta else 1, 1 if tb else 0)
        if nk == 1:
            o_ref[...] = p.astype(o_ref.dtype)
        else:
            acc = scratch[0]
            kk = pl.program_id(2)

            @pl.when(kk == 0)
            def _():
                acc[...] = p

            @pl.when(kk > 0)
            def _():
                acc[...] += p

            @pl.when(kk == nk - 1)
            def _():
                o_ref[...] = acc[...].astype(o_ref.dtype)

    a_spec = pl.BlockSpec((tk, tm), lambda i, j, l: (l, i)) if ta else pl.BlockSpec((tm, tk), lambda i, j, l: (i, l))
    b_spec = pl.BlockSpec((tn, tk), lambda i, j, l: (j, l)) if tb else pl.BlockSpec((tk, tn), lambda i, j, l: (l, j))
    return pl.pallas_call(
        body, name=name,
        out_shape=jax.ShapeDtypeStruct((m, n), out_dtype),
        grid=(m // tm, n // tn, nk),
        in_specs=[a_spec, b_spec],
        out_specs=pl.BlockSpec((tm, tn), lambda i, j, l: (i, j)),
        scratch_shapes=[pltpu.VMEM((tm, tn), F32)] if nk > 1 else [],
        compiler_params=_cparams(("parallel", "parallel", "arbitrary"), est),
    )(a, b)


def matmul(a, b, name):
    @jax.custom_vjp
    def op(a, b):
        return _mm_call(a, b, False, False, F32, name)

    def fwd(a, b):
        return op(a, b), (a, b)

    def bwd(res, g):
        a, b = res
        da = _mm_call(g, b, False, True, a.dtype, name + "_da")
        db = _mm_call(a, g, True, False, b.dtype, name + "_db")
        return da, db

    op.defvjp(fwd, bwd)
    return op(a, b)


def _row_tile(n_rows, bytes_per_row):
    ts = 1024
    while ts > SUBLANES and (ts * bytes_per_row > ROW_TILE_BYTES or n_rows % ts):
        ts //= 2
    return ts if n_rows % ts == 0 else n_rows


def _row_spec(ts, cols):
    return pl.BlockSpec((ts, cols), lambda i: (i, 0))


def _full_spec(shape):
    return pl.BlockSpec(shape, lambda i: (0,) * len(shape))


def _rowwise_fwd_call(f, rows, params, out_cols, name):
    n = rows[0].shape[0]
    per_row = 4 * (sum(r.shape[1] for r in rows) + sum(out_cols))
    ts = _row_tile(n, per_row)
    n_in, n_p = len(rows), len(params)

    def body(*refs):
        ins = [r[...] for r in refs[:n_in + n_p]]
        outs = f(*ins)
        for o_ref, o in zip(refs[n_in + n_p:], outs):
            o_ref[...] = o.astype(o_ref.dtype)

    return pl.pallas_call(
        body, name=name,
        out_shape=[jax.ShapeDtypeStruct((n, c), F32) for c in out_cols],
        grid=(n // ts,),
        in_specs=[_row_spec(ts, r.shape[1]) for r in rows] + [_full_spec(p.shape) for p in params],
        out_specs=[_row_spec(ts, c) for c in out_cols],
        compiler_params=_cparams(("parallel",), 2 * ts * per_row),
    )(*rows, *params)


def _rowwise_bwd_call(f, rows, params, cts, name):
    n = rows[0].shape[0]
    per_row = 4 * (2 * sum(r.shape[1] for r in rows) + 2 * sum(c.shape[1] for c in cts))
    ts = _row_tile(n, per_row)
    n_in, n_p, n_ct = len(rows), len(params), len(cts)

    def body(*refs):
        ins = [r[...] for r in refs[:n_in + n_p]]
        ct = tuple(r[...] for r in refs[n_in + n_p:n_in + n_p + n_ct])
        grads = jax.vjp(lambda *a: tuple(f(*a)), *ins)[1](ct)
        out_refs = refs[n_in + n_p + n_ct:]
        for o_ref, g in zip(out_refs[:n_in], grads[:n_in]):
            o_ref[...] = g
        first = pl.program_id(0) == 0
        for o_ref, g in zip(out_refs[n_in:], grads[n_in:]):
            @pl.when(first)
            def _(o_ref=o_ref, g=g):
                o_ref[...] = g

            @pl.when(jnp.logical_not(first))
            def _(o_ref=o_ref, g=g):
                o_ref[...] += g

    outs = pl.pallas_call(
        body, name=name,
        out_shape=[jax.ShapeDtypeStruct(r.shape, F32) for r in rows] + [jax.ShapeDtypeStruct(p.shape, F32) for p in params],
        grid=(n // ts,),
        in_specs=([_row_spec(ts, r.shape[1]) for r in rows] + [_full_spec(p.shape) for p in params]
                  + [_row_spec(ts, c.shape[1]) for c in cts]),
        out_specs=[_row_spec(ts, r.shape[1]) for r in rows] + [_full_spec(p.shape) for p in params],
        compiler_params=_cparams(("arbitrary",), 2 * ts * per_row),
    )(*rows, *params, *cts)
    return outs[:n_in], outs[n_in:]


def rowwise(f, rows, params, out_cols, name):
    @jax.custom_vjp
    def op(rows, params):
        return tuple(_rowwise_fwd_call(f, rows, params, out_cols, name))

    def fwd(rows, params):
        return op(rows, params), (rows, params)

    def bwd(res, cts):
        rows, params = res
        d_rows, d_params = _rowwise_bwd_call(f, rows, params, cts, name + "_bwd")
        return tuple(d_rows), tuple(d_params)

    op.defvjp(fwd, bwd)
    return op(tuple(rows), tuple(params))


def _rms_fn(x, g):
    return (x * lax.rsqrt(jnp.mean(x * x, axis=-1, keepdims=True) + RMS_EPS) * g,)


def _rope_fn(x1, x2, cos, sin):
    return x1 * cos - x2 * sin, x1 * sin + x2 * cos


def _log_sigmoid(x):
    return jnp.minimum(x, 0.0) - jnp.log1p(jnp.exp(-jnp.abs(x)))


def _forget_fn(f, bias):
    return (_log_sigmoid(f + bias),)


def _silu_gate_fn(y0, y1, y2, y3, gate_z):
    return tuple(y * (z * jax.nn.sigmoid(z))
                 for y, z in zip((y0, y1, y2, y3), (gate_z[:, n * BRANCH_WIDTH:(n + 1) * BRANCH_WIDTH] for n in range(N_BRANCHES))))


def _deepnorm_fn(x, out, g, b):
    h = DEEPNORM_ALPHA * x + out
    mu = jnp.mean(h, axis=-1, keepdims=True)
    var = jnp.mean(jnp.square(h - mu), axis=-1, keepdims=True)
    return ((h - mu) * lax.rsqrt(var + LN_EPS) * g + b,)


def _adamw_fn(g, w, m, v):
    m = ADAM_B1 * m + (1.0 - ADAM_B1) * g
    v = ADAM_B2 * v + (1.0 - ADAM_B2) * jnp.square(g)
    m_hat = m / (1.0 - ADAM_B1 ** ADAM_STEP)
    v_hat = v / (1.0 - ADAM_B2 ** ADAM_STEP)
    delta = -ADAM_LR * (m_hat / (jnp.sqrt(v_hat) + ADAM_EPS) + ADAM_WD * w)
    return delta, m, v


def split_cols(h, sizes):
    offs = np.cumsum((0,) + tuple(sizes))

    @jax.custom_vjp
    def op(h):
        return tuple(h[:, offs[i]:offs[i + 1]] for i in range(len(sizes)))

    def fwd(h):
        return op(h), None

    def bwd(_, cts):
        pad = h.shape[1] - int(offs[-1])
        parts = list(cts) + ([jnp.zeros((h.shape[0], pad), h.dtype)] if pad else [])
        return (jnp.concatenate(parts, axis=1),)

    op.defvjp(fwd, bwd)
    return op(h)


MERGE_ROWS = 256


def _merge_fwd_call(r, w_up, ps, name):
    n, rank = r.shape
    d = ps[0].shape[1]
    nb = len(ps)
    ts = min(MERGE_ROWS, n)

    def body(r_ref, w_ref, *refs):
        rb = r_ref[...].astype(MXU_DTYPE)
        acc = jnp.zeros((ts, d), F32)
        for i in range(nb):
            gate = jax.nn.sigmoid(_dot(rb, w_ref[:, i * d:(i + 1) * d]))
            acc = acc + gate * refs[i][...]
        refs[nb][...] = acc

    return pl.pallas_call(
        body, name=name,
        out_shape=jax.ShapeDtypeStruct((n, d), F32),
        grid=(n // ts,),
        in_specs=[_row_spec(ts, rank), _full_spec(w_up.shape)] + [_row_spec(ts, d)] * nb,
        out_specs=_row_spec(ts, d),
        compiler_params=_cparams(("parallel",), 2 * 4 * ts * d * (nb + 2)),
    )(r, w_up, *ps)


def _merge_bwd_call(r, w_up, ps, dm, name):
    n, rank = r.shape
    d = ps[0].shape[1]
    nb = len(ps)
    ts = min(MERGE_ROWS, n)
    steps = n // ts

    def body(r_ref, w_ref, *refs):
        p_refs, dm_ref = refs[:nb], refs[nb]
        dr_ref, dw_ref, dp_refs, dw_acc = refs[nb + 1], refs[nb + 2], refs[nb + 3:2 * nb + 3], refs[2 * nb + 3]
        step = pl.program_id(0)

        @pl.when(step == 0)
        def _():
            dw_acc[...] = jnp.zeros_like(dw_acc)

        rb = r_ref[...].astype(MXU_DTYPE)
        dmv = dm_ref[...]
        dr = jnp.zeros((ts, rank), F32)
        for i in range(nb):
            w_i = w_ref[:, i * d:(i + 1) * d]
            gate = jax.nn.sigmoid(_dot(rb, w_i))
            dp_refs[i][...] = dmv * gate
            d_logit = (dmv * p_refs[i][...] * gate * (1.0 - gate)).astype(MXU_DTYPE)
            dr = dr + _dot(d_logit, w_i, 1, 1)
            dw_acc[:, i * d:(i + 1) * d] += _dot(rb, d_logit, 0, 0)
        dr_ref[...] = dr

        @pl.when(step == steps - 1)
        def _():
            dw_ref[...] = dw_acc[...].astype(dw_ref.dtype)

    outs = pl.pallas_call(
        body, name=name,
        out_shape=[jax.ShapeDtypeStruct((n, rank), F32), jax.ShapeDtypeStruct(w_up.shape, w_up.dtype)]
        + [jax.ShapeDtypeStruct((n, d), F32)] * nb,
        grid=(steps,),
        in_specs=[_row_spec(ts, rank), _full_spec(w_up.shape)] + [_row_spec(ts, d)] * (nb + 1),
        out_specs=[_row_spec(ts, rank), _full_spec(w_up.shape)] + [_row_spec(ts, d)] * nb,
        scratch_shapes=[pltpu.VMEM(w_up.shape, F32)],
        compiler_params=_cparams(("arbitrary",), 2 * 4 * ts * d * (2 * nb + 2) + 8 * w_up.size),
    )(r, w_up, *ps, dm)
    return outs[0], outs[1], outs[2:]


def merge_gate(r, w_up, ps, name):
    @jax.custom_vjp
    def op(r, w_up, ps):
        return _merge_fwd_call(r, w_up, ps, name)

    def fwd(r, w_up, ps):
        return op(r, w_up, ps), (r, w_up, ps)

    def bwd(res, dm):
        r, w_up, ps = res
        dr, dw, dps = _merge_bwd_call(r, w_up, ps, dm, name + "_bwd")
        return dr, dw, tuple(dps)

    op.defvjp(fwd, bwd)
    return op(r, w_up, tuple(ps))


LOG2E = 1.4426950408889634


def _to_lanes(t):
    return t.transpose(0, 2, 1)


def _key_tiles(t, tk):
    h, s, d = t.shape
    return t.reshape(h, s // tk, tk, d).transpose(0, 1, 3, 2)


def _attn_vmem(sk, dq, dv, tq, tk, backward):
    resident = 2 * 2 * sk * (dq + dv) * (2 if backward else 1)
    grads = 2 * 4 * sk * (dq + dv) if backward else 0
    return resident + grads + 24 * tq * tk * 4


def _tile_ids(q0, k0, tq, tk):
    key = k0 + lax.broadcasted_iota(jnp.int32, (tk, tq), 0)
    qry = q0 + lax.broadcasted_iota(jnp.int32, (tk, tq), 1)
    return key, qry


def _lane_spec(d, tq):
    return pl.BlockSpec((1, d, tq), lambda hh, i: (hh, 0, i))


def _row_major_spec(tq, d):
    return pl.BlockSpec((1, tq, d), lambda hh, i: (hh, i, 0))


def _resident_spec(shape):
    return pl.BlockSpec((1,) + shape, lambda hh, i: (hh,) + (0,) * len(shape))


def _softmax_fwd_call(qt, k, vt, causal, name):
    h, dq, s = qt.shape
    sk, (n_kb, dv, tk) = k.shape[1], vt.shape[1:]
    tq = min(ATTN_TILE, s)
    assert not causal or (tq == tk and s == sk)

    def body(q_ref, k_ref, v_ref, o_ref, lse_ref):
        qi = pl.program_id(1)
        q_t = q_ref[0]

        def tile(kb, carry, masked):
            m, l, acc = carry
            off = pl.multiple_of(kb * tk, tk)
            st = _dot(k_ref[0, pl.ds(off, tk), :], q_t)
            if masked:
                key, qry = _tile_ids(qi * tq, kb * tk, tq, tk)
                st = jnp.where(key <= qry, st, -jnp.inf)
            m_new = jnp.maximum(m, jnp.max(st, axis=0, keepdims=True))
            alpha = jnp.exp2(m - m_new)
            p = jnp.exp2(st - m_new)
            l = alpha * l + jnp.sum(p, axis=0, keepdims=True)
            acc = alpha * acc + _dot(v_ref[0, kb], p.astype(MXU_DTYPE))
            return m_new, l, acc

        carry = (jnp.full((1, tq), -jnp.inf, F32), jnp.zeros((1, tq), F32), jnp.zeros((dv, tq), F32))
        carry = lax.fori_loop(0, qi if causal else n_kb, lambda kb, c: tile(kb, c, False), carry)
        if causal:
            carry = tile(qi, carry, True)
        m, l, acc = carry
        o_ref[0] = acc / l
        lse_ref[0] = m + jnp.log2(l)

    return pl.pallas_call(
        body, name=name,
        out_shape=[jax.ShapeDtypeStruct((h, dv, s), F32), jax.ShapeDtypeStruct((h, 1, s), F32)],
        grid=(h, s // tq),
        in_specs=[_lane_spec(dq, tq), _resident_spec((sk, dq)), _resident_spec((n_kb, dv, tk))],
        out_specs=[_lane_spec(dv, tq), _lane_spec(1, tq)],
        compiler_params=_cparams(("parallel", "arbitrary"), _attn_vmem(sk, dq, dv, tq, tk, False)),
    )(qt, k, vt)


def _softmax_bwd_call(qt, q, k, kt, v, ot, lse, dot, do, causal, name):
    h, dq, s = qt.shape
    sk, dv = k.shape[1], v.shape[2]
    n_kb, tk = kt.shape[1], kt.shape[3]
    tq = min(ATTN_TILE, s)

    def body(qt_ref, q_ref, k_ref, kt_ref, v_ref, ot_ref, lse_ref, dot_ref, do_ref, gq_ref, gk_ref, gv_ref):
        qi = pl.program_id(1)

        @pl.when(qi == 0)
        def _():
            gk_ref[...] = jnp.zeros_like(gk_ref)
            gv_ref[...] = jnp.zeros_like(gv_ref)

        q_t, q_n, do_n = qt_ref[0], q_ref[0], do_ref[0]
        do_t = dot_ref[0]
        do_tb = do_t.astype(MXU_DTYPE)
        delta = jnp.sum(do_t * ot_ref[0], axis=0, keepdims=True)
        lse_q = lse_ref[0]

        def tile(kb, gq_acc, masked):
            off = pl.multiple_of(kb * tk, tk)
            st = _dot(k_ref[0, pl.ds(off, tk), :], q_t)
            if masked:
                key, qry = _tile_ids(qi * tq, kb * tk, tq, tk)
                st = jnp.where(key <= qry, st, -jnp.inf)
            p = jnp.exp2(st - lse_q)
            dp = _dot(v_ref[0, pl.ds(off, tk), :], do_tb)
            ds = (p * (dp - delta)).astype(MXU_DTYPE)
            gk_ref[0, pl.ds(off, tk), :] += _dot(ds, q_n)
            gv_ref[0, pl.ds(off, tk), :] += _dot(p.astype(MXU_DTYPE), do_n)
            return gq_acc + _dot(kt_ref[0, kb], ds)

        gq = lax.fori_loop(0, qi if causal else n_kb, lambda kb, c: tile(kb, c, False), jnp.zeros((dq, tq), F32))
        if causal:
            gq = tile(qi, gq, True)
        gq_ref[0] = gq

    return pl.pallas_call(
        body, name=name,
        out_shape=[jax.ShapeDtypeStruct((h, dq, s), F32), jax.ShapeDtypeStruct((h, sk, dq), F32), jax.ShapeDtypeStruct((h, sk, dv), F32)],
        grid=(h, s // tq),
        in_specs=[_lane_spec(dq, tq), _row_major_spec(tq, dq), _resident_spec((sk, dq)), _resident_spec((n_kb, dq, tk)),
                  _resident_spec((sk, dv)), _lane_spec(dv, tq), _lane_spec(1, tq), _lane_spec(dv, tq), _row_major_spec(tq, dv)],
        out_specs=[_lane_spec(dq, tq), _resident_spec((sk, dq)), _resident_spec((sk, dv))],
        compiler_params=_cparams(("parallel", "arbitrary"), _attn_vmem(sk, dq, dv, tq, tk, True)),
    )(qt, q, k, kt, v, ot, lse, dot, do)


def _split_terms(x, n):
    info = jnp.finfo(MXU_DTYPE)
    terms = []
    for _ in range(n):
        t = lax.reduce_precision(x, info.nexp, info.nmant)
        terms.append(t)
        x = x - t
    return terms


def softmax_attention(q, k, v, c, scale, causal, name):
    decay = c is not None
    d = q.shape[2]
    tk = min(ATTN_TILE, k.shape[1])

    def widen(q, k, c):
        qs = q * (scale * LOG2E)
        if decay:
            terms = jnp.stack(_split_terms(c * LOG2E, 3), axis=-1)
            ones = jnp.ones_like(terms)
            pad = jnp.zeros(q.shape[:2] + (LANES - d - 6,), F32)
            qs = jnp.concatenate([qs, terms, ones, pad], axis=-1)
            k = jnp.concatenate([k, ones, -terms, pad], axis=-1)
        return qs.astype(MXU_DTYPE), k.astype(MXU_DTYPE)

    def run(q, k, v, c):
        qa, ka = widen(q, k, c)
        vb = v.astype(MXU_DTYPE)
        qt = _to_lanes(qa)
        ot, lse = _softmax_fwd_call(qt, ka, _key_tiles(vb, tk), causal, name)
        return _to_lanes(ot), (qt, qa, ka, vb, ot, lse)

    @jax.custom_vjp
    def op(q, k, v, c):
        return run(q, k, v, c)[0]

    def bwd(res, do):
        qt, qa, ka, vb, ot, lse = res
        gqt, gk, gv = _softmax_bwd_call(qt, qa, ka, _key_tiles(ka, tk), vb, ot, lse, _to_lanes(do),
                                        do.astype(MXU_DTYPE), causal, name + "_bwd")
        gq = _to_lanes(gqt)
        dc = gq[..., d] - gk[..., d + 3] if decay else None
        return gq[..., :d] * scale, gk[..., :d] * (1.0 / LOG2E), gv, dc

    op.defvjp(run, bwd)
    return op(q, k, v, c)


def _running_sum(ones, x, suffix):
    hi = x.astype(MXU_DTYPE)
    lo = (x - hi.astype(F32)).astype(MXU_DTYPE)
    b = ones.shape[0]
    n = x.shape[0] // b
    out, carry = [None] * n, None
    for i in (reversed(range(n)) if suffix else range(n)):
        blk = _dot(ones, hi[i * b:(i + 1) * b]) + _dot(ones, lo[i * b:(i + 1) * b])
        blk = blk if carry is None else blk + carry
        carry = blk[0:1, :] if suffix else blk[b - 1:b, :]
        out[i] = blk
    return out[0] if n == 1 else jnp.concatenate(out, axis=0)


def _ones_after(tk):
    b = min(tk, SUFFIX_BLOCK)
    return (lax.broadcasted_iota(jnp.int32, (b, b), 1) >= lax.broadcasted_iota(jnp.int32, (b, b), 0)).astype(MXU_DTYPE)


def _ones_before(tk):
    b = min(tk, SUFFIX_BLOCK)
    return (lax.broadcasted_iota(jnp.int32, (b, b), 1) <= lax.broadcasted_iota(jnp.int32, (b, b), 0)).astype(MXU_DTYPE)


LOGIT_CLAMP = 120.0


def _log_keep(k_tile, q_t, strict):
    z = jnp.minimum(_dot(k_tile, q_t), LOGIT_CLAMP)
    log_keep = -(jnp.log(1.0 + jnp.exp2(z)) * LOG2E)
    return z, (log_keep if strict is None else jnp.where(strict, log_keep, 0.0))


def _stick_weights(z, log_keep, later, ones_after, strict):
    local = _running_sum(ones_after, log_keep, True)
    a = jnp.exp2(z + local + later)
    return (a if strict is None else jnp.where(strict, a, 0.0)), local


def _strict_mask(qi, kb, t):
    key, qry = _tile_ids(qi * t, kb * t, t, t)
    return key < qry


def _stick_fwd_call(qt, k, vt, name):
    h, d, s = qt.shape
    t = vt.shape[3]
    n_t = s // t
    assert t == min(ATTN_TILE, s)

    def body(q_ref, k_ref, v_ref, o_ref, later_ref):
        qi = pl.program_id(1)
        q_t = q_ref[0]
        ones_after = _ones_after(t)
        later_ref[...] = jnp.zeros_like(later_ref)

        def tile(kb, carry, strict):
            later, acc = carry
            off = pl.multiple_of(kb * t, t)
            z, log_keep = _log_keep(k_ref[0, pl.ds(off, t), :], q_t, strict)
            a, local = _stick_weights(z, log_keep, later, ones_after, strict)
            later_ref[0, kb] = later
            return later + local[0:1, :], acc + _dot(v_ref[0, kb], a.astype(MXU_DTYPE))

        carry = tile(qi, (jnp.zeros((1, t), F32), jnp.zeros((d, t), F32)), _strict_mask(qi, qi, t))
        _, acc = lax.fori_loop(0, qi, lambda i, c: tile(qi - 1 - i, c, None), carry)
        o_ref[0] = acc

    later_spec = pl.BlockSpec((1, n_t, 1, t), lambda hh, i: (hh, 0, 0, i))
    return pl.pallas_call(
        body, name=name,
        out_shape=[jax.ShapeDtypeStruct((h, d, s), F32), jax.ShapeDtypeStruct((h, n_t, 1, s), F32)],
        grid=(h, n_t),
        in_specs=[_lane_spec(d, t), _resident_spec((s, d)), _resident_spec((n_t, d, t))],
        out_specs=[_lane_spec(d, t), later_spec],
        compiler_params=_cparams(("parallel", "arbitrary"), _attn_vmem(s, d, d, t, t, False)),
    )(qt, k, vt)


def _stick_bwd_call(qt, q, k, kt, v, later, dot, do, name):
    h, d, s = qt.shape
    t = kt.shape[3]
    n_t = s // t

    def body(qt_ref, q_ref, k_ref, kt_ref, v_ref, later_ref, dot_ref, do_ref, gq_ref, gk_ref, gv_ref):
        qi = pl.program_id(1)

        @pl.when(qi == 0)
        def _():
            gk_ref[...] = jnp.zeros_like(gk_ref)
            gv_ref[...] = jnp.zeros_like(gv_ref)

        q_t, q_n, do_n = qt_ref[0], q_ref[0], do_ref[0]
        do_tb = dot_ref[0].astype(MXU_DTYPE)
        ones_after, ones_before = _ones_after(t), _ones_before(t)
        diagonal = _strict_mask(qi, qi, t)

        def sweep_right(kb, carry, strict):
            before, gq_acc = carry
            off = pl.multiple_of(kb * t, t)
            z, log_keep = _log_keep(k_ref[0, pl.ds(off, t), :], q_t, strict)
            a, _ = _stick_weights(z, log_keep, later_ref[0, kb], ones_after, strict)
            g = a * _dot(v_ref[0, pl.ds(off, t), :], do_tb)
            g_prefix = _running_sum(ones_before, g, False) + before
            sig = 1.0 - jnp.exp2(log_keep)
            dz = g - sig * g_prefix
            dz = (dz if strict is None else jnp.where(strict, dz, 0.0)).astype(MXU_DTYPE)
            gk_ref[0, pl.ds(off, t), :] += _dot(dz, q_n)
            gv_ref[0, pl.ds(off, t), :] += _dot(a.astype(MXU_DTYPE), do_n)
            return g_prefix[t - 1:t, :], gq_acc + _dot(kt_ref[0, kb], dz)

        carry = lax.fori_loop(0, qi, lambda kb, c: sweep_right(kb, c, None), (jnp.zeros((1, t), F32), jnp.zeros((d, t), F32)))
        _, gq = sweep_right(qi, carry, diagonal)
        gq_ref[0] = gq

    return pl.pallas_call(
        body, name=name,
        out_shape=[jax.ShapeDtypeStruct((h, d, s), F32), jax.ShapeDtypeStruct((h, s, d), F32), jax.ShapeDtypeStruct((h, s, d), F32)],
        grid=(h, n_t),
        in_specs=[_lane_spec(d, t), _row_major_spec(t, d), _resident_spec((s, d)), _resident_spec((n_t, d, t)),
                  _resident_spec((s, d)), pl.BlockSpec((1, n_t, 1, t), lambda hh, i: (hh, 0, 0, i)),
                  _lane_spec(d, t), _row_major_spec(t, d)],
        out_specs=[_lane_spec(d, t), _resident_spec((s, d)), _resident_spec((s, d))],
        compiler_params=_cparams(("parallel", "arbitrary"), _attn_vmem(s, d, d, t, t, True)),
    )(qt, q, k, kt, v, later, dot, do)


def stick_breaking_attention(q, k, v, scale, name):
    t = min(ATTN_TILE, q.shape[1])

    def run(q, k, v):
        qb, kb, vb = (q * (scale * LOG2E)).astype(MXU_DTYPE), k.astype(MXU_DTYPE), v.astype(MXU_DTYPE)
        qt = _to_lanes(qb)
        ot, later = _stick_fwd_call(qt, kb, _key_tiles(vb, t), name)
        return _to_lanes(ot), (qt, qb, kb, vb, later)

    @jax.custom_vjp
    def op(q, k, v):
        return run(q, k, v)[0]

    def bwd(res, do):
        qt, qb, kb, vb, later = res
        gqt, gk, gv = _stick_bwd_call(qt, qb, kb, _key_tiles(kb, t), vb, later, _to_lanes(do), do.astype(MXU_DTYPE), name + "_bwd")
        return _to_lanes(gqt) * scale, gk * (1.0 / LOG2E), gv

    op.defvjp(run, bwd)
    return op(q, k, v)


def _cumsum_call(x, reverse, name):
    r, s = x.shape
    tb = min(ATTN_TILE, s)
    nb = s // tb

    def body(x_ref, o_ref, carry):
        @pl.when(pl.program_id(0) == 0)
        def _():
            carry[...] = jnp.zeros_like(carry)

        j = lax.broadcasted_iota(jnp.int32, (tb, tb), 0)
        t = lax.broadcasted_iota(jnp.int32, (tb, tb), 1)
        ones = ((j >= t) if reverse else (j <= t)).astype(MXU_DTYPE)
        rest = x_ref[...]
        y = carry[:, 0:1]
        for _ in range(3):
            term = rest.astype(MXU_DTYPE)
            rest = rest - term.astype(F32)
            y = y + _dot(term, ones)
        o_ref[...] = y
        carry[...] = jnp.broadcast_to(y[:, 0:1] if reverse else y[:, tb - 1:tb], carry.shape)

    spec = pl.BlockSpec((r, tb), (lambda i: (0, nb - 1 - i)) if reverse else (lambda i: (0, i)))
    return pl.pallas_call(
        body, name=name,
        out_shape=jax.ShapeDtypeStruct((r, s), F32),
        grid=(nb,),
        in_specs=[spec], out_specs=spec,
        scratch_shapes=[pltpu.VMEM((r, LANES), F32)],
        compiler_params=_cparams(("arbitrary",), 1024 * 1024),
    )(x)


def cumsum_lanes(x, name):
    @jax.custom_vjp
    def op(x):
        return _cumsum_call(x, False, name)

    op.defvjp(lambda x: (op(x), None), lambda _, g: (_cumsum_call(g, True, name + "_bwd"),))
    return op(x)


def _loss_call(y, target):
    n, d = y.shape
    ts = _row_tile(n, 4 * 3 * d)

    def body(y_ref, t_ref, dy_ref, loss_ref):
        err = y_ref[...] - t_ref[...]
        dy_ref[...] = err * (1.0 / d)
        part = jnp.sum(jnp.sum(err * err, axis=1, keepdims=True), axis=0, keepdims=True) * (0.5 / d)

        @pl.when(pl.program_id(0) == 0)
        def _():
            loss_ref[...] = jnp.zeros_like(loss_ref)

        loss_ref[...] += jnp.broadcast_to(part, loss_ref.shape)

    dy, loss = pl.pallas_call(
        body, name="loss_head",
        out_shape=[jax.ShapeDtypeStruct((n, d), F32), jax.ShapeDtypeStruct((SUBLANES, LANES), F32)],
        grid=(n // ts,),
        in_specs=[_row_spec(ts, d), _row_spec(ts, d)],
        out_specs=[_row_spec(ts, d), _full_spec((SUBLANES, LANES))],
        compiler_params=_cparams(("arbitrary",), 2 * ts * 4 * 3 * d),
    )(y, target)
    return loss[0, 0], dy


_FLIPS = [(0, 0, 1), (0, 1, 0), (0, 1, 1), (1, 0, 0), (1, 0, 1), (1, 1, 0), (1, 1, 1)]
MESH = pl.DeviceIdType.MESH
ANY_SPEC = pl.BlockSpec(memory_space=pl.ANY)


def _me_and_peers():
    x, y, c = lax.axis_index("x"), lax.axis_index("y"), lax.axis_index("c")
    peers = [((1 - x) if fx else x, (1 - y) if fy else y, (1 - c) if fc else c) for fx, fy, fc in _FLIPS]
    return 4 * x + 2 * y + c, peers, [4 * px + 2 * py + pc for px, py, pc in peers]


def all_gather_hbm(shard, name):
    def body(x_ref, out_ref, send_sems, recv_sems, local_sem):
        me, peers, peer_ids = _me_and_peers()
        mine = pltpu.make_async_copy(x_ref, out_ref.at[me], local_sem)
        mine.start()
        sends = [pltpu.make_async_remote_copy(src_ref=x_ref, dst_ref=out_ref.at[me], send_sem=send_sems.at[i],
                                              recv_sem=recv_sems.at[i], device_id=p, device_id_type=MESH)
                 for i, p in enumerate(peers)]
        for cp in sends:
            cp.start()
        for i, (p, pid) in enumerate(zip(peers, peer_ids)):
            pltpu.make_async_remote_copy(src_ref=x_ref, dst_ref=out_ref.at[pid], send_sem=send_sems.at[i],
                                         recv_sem=recv_sems.at[i], device_id=p, device_id_type=MESH).wait_recv()
        for cp in sends:
            cp.wait_send()
        mine.wait()

    return pl.pallas_call(
        body, name=name,
        out_shape=jax.ShapeDtypeStruct((N_DEV,) + shard.shape, shard.dtype),
        in_specs=[ANY_SPEC], out_specs=ANY_SPEC,
        scratch_shapes=[pltpu.SemaphoreType.DMA((N_DEV - 1,)), pltpu.SemaphoreType.DMA((N_DEV - 1,)), pltpu.SemaphoreType.DMA],
    )(shard)


def all_to_all_hbm(blocks, name):
    def body(x_ref, out_ref, send_sems, recv_sems, local_sem):
        me, peers, peer_ids = _me_and_peers()
        mine = pltpu.make_async_copy(x_ref.at[me], out_ref.at[me], local_sem)
        mine.start()
        sends = [pltpu.make_async_remote_copy(src_ref=x_ref.at[pid], dst_ref=out_ref.at[me], send_sem=send_sems.at[i],
                                              recv_sem=recv_sems.at[i], device_id=p, device_id_type=MESH)
                 for i, (p, pid) in enumerate(zip(peers, peer_ids))]
        for cp in sends:
            cp.start()
        for i, (p, pid) in enumerate(zip(peers, peer_ids)):
            pltpu.make_async_remote_copy(src_ref=x_ref.at[me], dst_ref=out_ref.at[pid], send_sem=send_sems.at[i],
                                         recv_sem=recv_sems.at[i], device_id=p, device_id_type=MESH).wait_recv()
        for cp in sends:
            cp.wait_send()
        mine.wait()

    return pl.pallas_call(
        body, name=name,
        out_shape=jax.ShapeDtypeStruct(blocks.shape, blocks.dtype),
        in_specs=[ANY_SPEC], out_specs=ANY_SPEC,
        scratch_shapes=[pltpu.SemaphoreType.DMA((N_DEV - 1,)), pltpu.SemaphoreType.DMA((N_DEV - 1,)), pltpu.SemaphoreType.DMA],
    )(blocks)


def sum_blocks(blocks, name):
    _, r, c = blocks.shape
    ts = _row_tile(r, (N_DEV * blocks.dtype.itemsize + 4) * c)

    def body(x_ref, o_ref):
        acc = x_ref[0].astype(F32)
        for d in range(1, N_DEV):
            acc = acc + x_ref[d].astype(F32)
        o_ref[...] = acc

    return pl.pallas_call(
        body, name=name,
        out_shape=jax.ShapeDtypeStruct((r, c), F32),
        grid=(r // ts,),
        in_specs=[pl.BlockSpec((N_DEV, ts, c), lambda i: (0, i, 0))],
        out_specs=_row_spec(ts, c),
        compiler_params=_cparams(("parallel",), 2 * ts * c * (N_DEV * blocks.dtype.itemsize + 4)),
    )(blocks)


def all_reduce_small(v, name):
    r, c = v.shape

    def body(x_ref, out_ref, gathered, send_sems, recv_sems):
        me, peers, peer_ids = _me_and_peers()
        sends = [pltpu.make_async_remote_copy(src_ref=x_ref, dst_ref=gathered.at[me], send_sem=send_sems.at[i],
                                              recv_sem=recv_sems.at[i], device_id=p, device_id_type=MESH)
                 for i, p in enumerate(peers)]
        for cp in sends:
            cp.start()
        gathered[me] = x_ref[...]
        for i, (p, pid) in enumerate(zip(peers, peer_ids)):
            pltpu.make_async_remote_copy(src_ref=x_ref, dst_ref=gathered.at[pid], send_sem=send_sems.at[i],
                                         recv_sem=recv_sems.at[i], device_id=p, device_id_type=MESH).wait_recv()
        for cp in sends:
            cp.wait_send()
        acc = gathered[0]
        for d in range(1, N_DEV):
            acc = acc + gathered[d]
        out_ref[...] = acc

    return pl.pallas_call(
        body, name=name,
        out_shape=jax.ShapeDtypeStruct((r, c), F32),
        in_specs=[pl.BlockSpec(memory_space=pltpu.VMEM)],
        out_specs=pl.BlockSpec(memory_space=pltpu.VMEM),
        scratch_shapes=[pltpu.VMEM((N_DEV, r, c), F32), pltpu.SemaphoreType.DMA((N_DEV - 1,)), pltpu.SemaphoreType.DMA((N_DEV - 1,))],
    )(v)


_SHARD_AXIS = {"w_in": 1, "mla_w_qb": 1, "mla_w_kvb": 2, "w_mem_kv": 1, "w_merge_up": 2, "w_branch": 3, "w_out": 1}
_SMALL = ("mla_q_norm", "mla_kv_norm", "fox_forget_bias", "ln_gain", "ln_bias")

def _regroup_qb(w):
    lead = w.shape[:-1]
    t = w.reshape(lead + (N_HEADS, MLA_NOPE + MLA_ROPE))
    half = MLA_ROPE // 2
    parts = (t[..., :MLA_NOPE], t[..., MLA_NOPE:MLA_NOPE + half], t[..., MLA_NOPE + half:])
    return jnp.concatenate([p.reshape(lead + (-1,)) for p in parts], axis=-1)


def _ungroup_qb(g):
    lead = g.shape[:-1]
    half = MLA_ROPE // 2
    n_nope, n_half = N_HEADS * MLA_NOPE, N_HEADS * half
    parts = (g[..., :n_nope].reshape(lead + (N_HEADS, MLA_NOPE)), g[..., n_nope:n_nope + n_half].reshape(lead + (N_HEADS, half)),
             g[..., n_nope + n_half:].reshape(lead + (N_HEADS, half)))
    return jnp.concatenate(parts, axis=-1).reshape(lead + (-1,))


def _to_wire(name, w):
    if name == "w_in":
        return jnp.pad(w, ((0, 0), (0, 0), (0, IN_WIDTH_PAD - IN_WIDTH)))
    if name == "mla_w_qb":
        return _regroup_qb(w)
    return w


def _from_wire(name, g):
    if name == "w_in":
        return g[:, :, :IN_WIDTH]
    if name == "mla_w_qb":
        return _ungroup_qb(g)
    return g


def _join(name, blocks):
    a = _SHARD_AXIS[name]
    t = jnp.moveaxis(blocks, 0, a)
    return t.reshape(t.shape[:a] + (t.shape[a] * t.shape[a + 1],) + t.shape[a + 2:])


def _cut(name, full):
    a = _SHARD_AXIS[name]
    t = full.reshape(full.shape[:a] + (N_DEV, full.shape[a] // N_DEV) + full.shape[a + 1:])
    return jnp.moveaxis(t, a, 0)


def _heads(t):
    s = t.shape[0]
    return t.reshape(s, N_HEADS, -1).transpose(1, 0, 2)


def _unheads(t):
    return t.transpose(1, 0, 2).reshape(t.shape[1], -1)


def _layer(x, mem, rope, w, p, tag):
    s = x.shape[0]
    cos, sin, cos4, sin4 = rope
    h = matmul(x, w["w_in"], tag + "in_proj")
    c_q, c_kv, k_rope, sb_qkv, fox_qkv, fox_f, mem_q, gate_z, merge_r = split_cols(h, IN_SPLITS)

    (cq_n,) = rowwise(_rms_fn, [c_q], [p["mla_q_norm"]], [MLA_Q_RANK], tag + "q_rms")
    q_all = matmul(cq_n, w["mla_w_qb"], tag + "q_up")
    n_nope, n_half = N_HEADS * MLA_NOPE, N_HEADS * MLA_ROPE // 2
    q_nope, q_x1, q_x2 = q_all[:, :n_nope], q_all[:, n_nope:n_nope + n_half], q_all[:, n_nope + n_half:]
    (ckv_n,) = rowwise(_rms_fn, [c_kv], [p["mla_kv_norm"]], [MLA_KV_RANK], tag + "kv_rms")
    kv = matmul(ckv_n, w["mla_w_kvb"], tag + "kv_up").reshape(s, N_HEADS, MLA_NOPE + HEAD_DIM)
    k_nope, v_mla = kv[..., :MLA_NOPE], kv[..., MLA_NOPE:]
    half = MLA_ROPE // 2
    q_o1, q_o2 = rowwise(_rope_fn, [q_x1, q_x2, cos4, sin4], [], [n_half, n_half], tag + "q_rope")
    k_o1, k_o2 = rowwise(_rope_fn, [k_rope[:, :half], k_rope[:, half:], cos, sin], [], [half, half], tag + "k_rope")
    zeros = jnp.zeros((s, N_HEADS, MLA_QK_PAD - MLA_NOPE - MLA_ROPE), F32)
    q_mla = jnp.concatenate([q_nope.reshape(s, N_HEADS, MLA_NOPE), q_o1.reshape(s, N_HEADS, half),
                             q_o2.reshape(s, N_HEADS, half), zeros], axis=-1).transpose(1, 0, 2)
    k_mla = jnp.concatenate([k_nope, jnp.broadcast_to(k_o1[:, None, :], (s, N_HEADS, half)),
                             jnp.broadcast_to(k_o2[:, None, :], (s, N_HEADS, half)), zeros], axis=-1).transpose(1, 0, 2)
    y_mla = softmax_attention(q_mla, k_mla, v_mla.transpose(1, 0, 2), None,
                              (MLA_NOPE + MLA_ROPE) ** -0.5, True, tag + "mla_attn")

    width = N_HEADS * HEAD_DIM
    sq, sk, sv = (_heads(sb_qkv[:, i * width:(i + 1) * width]) for i in range(3))
    y_sb = stick_breaking_attention(sq, sk, sv, HEAD_DIM ** -0.5, tag + "stick_attn")

    fq, fk, fv = (_heads(fox_qkv[:, i * width:(i + 1) * width]) for i in range(3))
    (log_f,) = rowwise(_forget_fn, [fox_f], [p["fox_forget_bias"]], [N_HEADS], tag + "forget_gate")
    log_f8 = jnp.pad(log_f.T, ((0, SUBLANES - N_HEADS), (0, 0)))
    c = cumsum_lanes(log_f8, tag + "forget_cumsum")[:N_HEADS]
    y_fox = softmax_attention(fq, fk, fv, c, HEAD_DIM ** -0.5, True, tag + "fox_attn")

    mkv = matmul(mem, w["w_mem_kv"], tag + "mem_kv")
    y_mem = softmax_attention(_heads(mem_q), _heads(mkv[:, :width]), _heads(mkv[:, width:]), None,
                              HEAD_DIM ** -0.5, False, tag + "mem_attn")

    ys = [_unheads(t) for t in (y_mla, y_sb, y_fox, y_mem)]
    yb = rowwise(_silu_gate_fn, ys + [gate_z], [], [BRANCH_WIDTH] * N_BRANCHES, tag + "silu_gate")
    proj = [matmul(yb[n], w["w_branch"][n], tag + "branch%d" % n) for n in range(N_BRANCHES)]
    merged = merge_gate(merge_r, w["w_merge_up"], proj, tag + "merge")
    out = matmul(merged, w["w_out"], tag + "out_proj")
    (xn,) = rowwise(_deepnorm_fn, [x, out], [p["ln_gain"], p["ln_bias"]], [x.shape[1]], tag + "deepnorm")
    return xn


def _adamw(g, w, m, v, name):
    shape = g.shape
    two_d = (-1, shape[-1])
    g2, w2, m2, v2 = (t.reshape(two_d) for t in (g, w, m, v))
    delta, new_m, new_v = _rowwise_fwd_call(_adamw_fn, [g2, w2, m2, v2], [], [shape[-1]] * 3, name)
    return delta.reshape(shape), new_m.reshape(shape), new_v.reshape(shape)


def kernel(x, mem, positions, w_in, mla_q_norm, mla_w_qb, mla_kv_norm, mla_w_kvb, fox_forget_bias, w_mem_kv, w_merge_up, w_branch, w_out, ln_gain, ln_bias, loss_target, m_w_in, m_mla_q_norm, m_mla_w_qb, m_mla_kv_norm, m_mla_w_kvb, m_fox_forget_bias, m_w_mem_kv, m_w_merge_up, m_w_branch, m_w_out, m_ln_gain, m_ln_bias, v_w_in, v_mla_q_norm, v_mla_w_qb, v_mla_kv_norm, v_mla_w_kvb, v_fox_forget_bias, v_w_mem_kv, v_w_merge_up, v_w_branch, v_w_out, v_ln_gain, v_ln_bias):
    weights = dict(w_in=w_in, mla_q_norm=mla_q_norm, mla_w_qb=mla_w_qb, mla_kv_norm=mla_kv_norm, mla_w_kvb=mla_w_kvb,
                   fox_forget_bias=fox_forget_bias, w_mem_kv=w_mem_kv, w_merge_up=w_merge_up, w_branch=w_branch,
                   w_out=w_out, ln_gain=ln_gain, ln_bias=ln_bias)
    m_in = dict(w_in=m_w_in, mla_q_norm=m_mla_q_norm, mla_w_qb=m_mla_w_qb, mla_kv_norm=m_mla_kv_norm, mla_w_kvb=m_mla_w_kvb,
                fox_forget_bias=m_fox_forget_bias, w_mem_kv=m_w_mem_kv, w_merge_up=m_w_merge_up, w_branch=m_w_branch,
                w_out=m_w_out, ln_gain=m_ln_gain, ln_bias=m_ln_bias)
    v_in = dict(w_in=v_w_in, mla_q_norm=v_mla_q_norm, mla_w_qb=v_mla_w_qb, mla_kv_norm=v_mla_kv_norm, mla_w_kvb=v_mla_w_kvb,
                fox_forget_bias=v_fox_forget_bias, w_mem_kv=v_w_mem_kv, w_merge_up=v_w_merge_up, w_branch=v_w_branch,
                w_out=v_w_out, ln_gain=v_ln_gain, ln_bias=v_ln_bias)
    order = list(weights)
    big = list(_SHARD_AXIS)
    n_layers = w_in.shape[0]
    x0, mem0 = x[0], mem[0]
    s = x0.shape[0]

    wire = [_to_wire(n, weights[n]).astype(WIRE_DTYPE) for n in big]
    sizes = [t.size for t in wire]
    flat = jnp.concatenate([t.reshape(-1) for t in wire]).reshape(-1, LANES)
    gathered = all_gather_hbm(flat, "gather_weights").reshape(N_DEV, -1)
    offs = np.cumsum([0] + sizes)
    full = {n: _join(n, gathered[:, offs[i]:offs[i + 1]].reshape((N_DEV,) + wire[i].shape)) for i, n in enumerate(big)}
    per_layer_w = [{n: full[n][l] for n in big} for l in range(n_layers)]
    per_layer_p = [{n: weights[n][l][None, :] for n in _SMALL} for l in range(n_layers)]

    inv_freq = ROPE_THETA ** (-jnp.arange(0, MLA_ROPE, 2, dtype=F32) / MLA_ROPE)
    ang = positions[0].astype(F32)[:, None] * inv_freq
    cos, sin = jnp.cos(ang), jnp.sin(ang)
    rope = (cos, sin, jnp.tile(cos, (1, N_HEADS)), jnp.tile(sin, (1, N_HEADS)))

    def trunk(x, ws, ps):
        for l in range(n_layers):
            x = _layer(x, mem0, rope, ws[l], ps[l], "l%d_" % l)
        return x

    y, pullback = jax.vjp(trunk, x0, per_layer_w, per_layer_p)
    loss_part, dy = _loss_call(y, loss_target[0])
    grad_x, grad_ws, grad_ps = pullback(dy)

    cut = [_cut(n, jnp.stack([grad_ws[l][n] for l in range(n_layers)])) for n in big]
    blocks = jnp.concatenate([t.reshape(N_DEV, -1) for t in cut], axis=1).reshape(N_DEV, -1, LANES)
    received = all_to_all_hbm(blocks, "scatter_grads")
    summed = sum_blocks(received, "sum_grads").reshape(-1)
    grads = {n: _from_wire(n, summed[offs[i]:offs[i + 1]].reshape(wire[i].shape)) for i, n in enumerate(big)}

    small = [jnp.stack([grad_ps[l][n][0] for l in range(n_layers)]) for n in _SMALL]
    small_sizes = [t.size for t in small]
    packed = jnp.concatenate([t.reshape(-1) for t in small] + [loss_part.reshape(1)])
    rows = -(-packed.size // (SUBLANES * LANES)) * SUBLANES
    packed = jnp.pad(packed, (0, rows * LANES - packed.size)).reshape(rows, LANES)
    reduced = all_reduce_small(packed, "reduce_small").reshape(-1)
    small_offs = np.cumsum([0] + small_sizes)
    for i, n in enumerate(_SMALL):
        grads[n] = reduced[small_offs[i]:small_offs[i + 1]].reshape(small[i].shape)
    loss = reduced[small_offs[-1]]

    delta, new_m, new_v = {}, {}, {}
    for n in order:
        delta[n], new_m[n], new_v[n] = _adamw(grads[n], weights[n], m_in[n], v_in[n], "adamw_" + n)
    return (loss, grad_x[None], *[grads[n] for n in order], *[delta[n] for n in order],
            *[new_m[n] for n in order], *[new_v[n] for n in order])
```

```python
import functools

import numpy as np
import jax
import jax.numpy as jnp
from jax import lax
from jax.experimental import pallas as pl
from jax.experimental.pallas import tpu as pltpu

F32 = jnp.float32
MXU_DTYPE = jnp.bfloat16
WIRE_DTYPE = jnp.bfloat16

N_LAYERS = 4
N_BRANCHES = 4
BRANCH_WIDTH = 256
N_HEADS = 4
HEAD_DIM = 64
MLA_NOPE = 64
MLA_ROPE = 32
MLA_Q_RANK = 384
MLA_KV_RANK = 256
ROPE_THETA = 10000.0
MERGE_RANK = 128
RMS_EPS = 1e-6
LN_EPS = 1e-5
DEEPNORM_ALPHA = (2 * N_LAYERS) ** 0.25
IN_SPLITS = (MLA_Q_RANK, MLA_KV_RANK, MLA_ROPE, 3 * N_HEADS * HEAD_DIM, 3 * N_HEADS * HEAD_DIM,
             N_HEADS, N_HEADS * HEAD_DIM, N_BRANCHES * BRANCH_WIDTH, MERGE_RANK)
IN_WIDTH = sum(IN_SPLITS)
ADAM_LR, ADAM_B1, ADAM_B2, ADAM_EPS, ADAM_WD, ADAM_STEP = 0.001, 0.9, 0.999, 1e-08, 0.01, 10

LANES = 128
SUBLANES = 8
VMEM_BYTES_V7X = 64 * 1024 * 1024
VMEM_LIMIT_CAP = VMEM_BYTES_V7X - 8 * 1024 * 1024
MM_VMEM_BUDGET = 44 * 1024 * 1024
ROW_TILE_BYTES = 4 * 1024 * 1024
ATTN_TILE = 512
SUFFIX_BLOCK = 256
N_DEV = 8
IN_WIDTH_PAD = -(-IN_WIDTH // LANES) * LANES
MLA_QK_PAD = 128


def _cparams(semantics, vmem_estimate):
    limit = int(min(max(2 * vmem_estimate, 32 * 1024 * 1024), VMEM_LIMIT_CAP))
    return pltpu.CompilerParams(dimension_semantics=semantics, vmem_limit_bytes=limit)


def _dot(a, b, ca=1, cb=0):
    return lax.dot_general(a, b, (((ca,), (cb,)), ((), ())), preferred_element_type=F32)


def _div_tile(n, cap, align):
    t = (cap // align) * align
    while t >= align:
        if n % t == 0:
            return t
        t -= align
    return n


def _mm_call(a, b, ta, tb, out_dtype, name):
    k, m = a.shape if ta else a.shape[::-1]
    n, kb = b.shape if tb else b.shape[::-1]
    assert k == kb, (a.shape, b.shape, ta, tb)
    ab, bb, ob = a.dtype.itemsize, b.dtype.itemsize, jnp.dtype(out_dtype).itemsize
    tk = k if k <= 4096 else 512
    tn = n if n <= 4096 else _div_tile(n, 2048, LANES)
    cands = [t for t in (1024, 512, 256, 128) if t <= m and m % t == 0] or [m]
    for tm in cands:
        est = 2 * (tm * tk * ab + tk * tn * bb + tm * tn * ob) + (tm * tn * 4 if k > tk else 0)
        if est <= MM_VMEM_BUDGET:
            break
    nk = k // tk

    def body(a_ref, b_ref, o_ref, *scratch):
        p = _dot(a_ref[...].astype(MXU_DTYPE), b_ref[...].astype(MXU_DTYPE), 0 if ta else 1, 1 if tb else 0)
        if nk == 1:
            o_ref[...] = p.astype(o_ref.dtype)
        else:
            acc = scratch[0]
            kk = pl.program_id(2)

            @pl.when(kk == 0)
            def _():
                acc[...] = p

            @pl.when(kk > 0)
            def _():
                acc[...] += p

            @pl.when(kk == nk - 1)
            def _():
                o_ref[...] = acc[...].astype(o_ref.dtype)

    a_spec = pl.BlockSpec((tk, tm), lambda i, j, l: (l, i)) if ta else pl.BlockSpec((tm, tk), lambda i, j, l: (i, l))
    b_spec = pl.BlockSpec((tn, tk), lambda i, j, l: (j, l)) if tb else pl.BlockSpec((tk, tn), lambda i, j, l: (l, j))
    return pl.pallas_call(
        body, name=name,
        out_shape=jax.ShapeDtypeStruct((m, n), out_dtype),
        grid=(m // tm, n // tn, nk),
        in_specs=[a_spec, b_spec],
        out_specs=pl.BlockSpec((tm, tn), lambda i, j, l: (i, j)),
        scratch_shapes=[pltpu.VMEM((tm, tn), F32)] if nk > 1 else [],
        compiler_params=_cparams(("parallel", "parallel", "arbitrary"), est),
    )(a, b)


def matmul(a, b, name):
    @jax.custom_vjp
    def op(a, b):
        return _mm_call(a, b, False, False, F32, name)

    def fwd(a, b):
        return op(a, b), (a, b)

    def bwd(res, g):
        a, b = res
        da = _mm_call(g, b, False, True, a.dtype, name + "_da")
        db = _mm_call(a, g, True, False, b.dtype, name + "_db")
        return da, db

    op.defvjp(fwd, bwd)
    return op(a, b)


def _row_tile(n_rows, bytes_per_row):
    ts = 1024
    while ts > SUBLANES and (ts * bytes_per_row > ROW_TILE_BYTES or n_rows % ts):
        ts //= 2
    return ts if n_rows % ts == 0 else n_rows


def _row_spec(ts, cols):
    return pl.BlockSpec((ts, cols), lambda i: (i, 0))


def _full_spec(shape):
    return pl.BlockSpec(shape, lambda i: (0,) * len(shape))


def _rowwise_fwd_call(f, rows, params, out_cols, name):
    n = rows[0].shape[0]
    per_row = 4 * (sum(r.shape[1] for r in rows) + sum(out_cols))
    ts = _row_tile(n, per_row)
    n_in, n_p = len(rows), len(params)

    def body(*refs):
        ins = [r[...] for r in refs[:n_in + n_p]]
        outs = f(*ins)
        for o_ref, o in zip(refs[n_in + n_p:], outs):
            o_ref[...] = o.astype(o_ref.dtype)

    return pl.pallas_call(
        body, name=name,
        out_shape=[jax.ShapeDtypeStruct((n, c), F32) for c in out_cols],
        grid=(n // ts,),
        in_specs=[_row_spec(ts, r.shape[1]) for r in rows] + [_full_spec(p.shape) for p in params],
        out_specs=[_row_spec(ts, c) for c in out_cols],
        compiler_params=_cparams(("parallel",), 2 * ts * per_row),
    )(*rows, *params)


def _rowwise_bwd_call(f, rows, params, cts, name):
    n = rows[0].shape[0]
    per_row = 4 * (2 * sum(r.shape[1] for r in rows) + 2 * sum(c.shape[1] for c in cts))
    ts = _row_tile(n, per_row)
    n_in, n_p, n_ct = len(rows), len(params), len(cts)

    def body(*refs):
        ins = [r[...] for r in refs[:n_in + n_p]]
        ct = tuple(r[...] for r in refs[n_in + n_p:n_in + n_p + n_ct])
        grads = jax.vjp(lambda *a: tuple(f(*a)), *ins)[1](ct)
        out_refs = refs[n_in + n_p + n_ct:]
        for o_ref, g in zip(out_refs[:n_in], grads[:n_in]):
            o_ref[...] = g
        first = pl.program_id(0) == 0
        for o_ref, g in zip(out_refs[n_in:], grads[n_in:]):
            @pl.when(first)
            def _(o_ref=o_ref, g=g):
                o_ref[...] = g

            @pl.when(jnp.logical_not(first))
            def _(o_ref=o_ref, g=g):
                o_ref[...] += g

    outs = pl.pallas_call(
        body, name=name,
        out_shape=[jax.ShapeDtypeStruct(r.shape, F32) for r in rows] + [jax.ShapeDtypeStruct(p.shape, F32) for p in params],
        grid=(n // ts,),
        in_specs=([_row_spec(ts, r.shape[1]) for r in rows] + [_full_spec(p.shape) for p in params]
                  + [_row_spec(ts, c.shape[1]) for c in cts]),
        out_specs=[_row_spec(ts, r.shape[1]) for r in rows] + [_full_spec(p.shape) for p in params],
        compiler_params=_cparams(("arbitrary",), 2 * ts * per_row),
    )(*rows, *params, *cts)
    return outs[:n_in], outs[n_in:]


def rowwise(f, rows, params, out_cols, name):
    @jax.custom_vjp
    def op(rows, params):
        return tuple(_rowwise_fwd_call(f, rows, params, out_cols, name))

    def fwd(rows, params):
        return op(rows, params), (rows, params)

    def bwd(res, cts):
        rows, params = res
        d_rows, d_params = _rowwise_bwd_call(f, rows, params, cts, name + "_bwd")
        return tuple(d_rows), tuple(d_params)

    op.defvjp(fwd, bwd)
    return op(tuple(rows), tuple(params))


def _rms_fn(x, g):
    return (x * lax.rsqrt(jnp.mean(x * x, axis=-1, keepdims=True) + RMS_EPS) * g,)


def _rope_fn(x1, x2, cos, sin):
    return x1 * cos - x2 * sin, x1 * sin + x2 * cos


def _log_sigmoid(x):
    return jnp.minimum(x, 0.0) - jnp.log1p(jnp.exp(-jnp.abs(x)))


def _forget_fn(f, bias):
    return (_log_sigmoid(f + bias),)


def _silu_gate_fn(y0, y1, y2, y3, gate_z):
    return tuple(y * (z * jax.nn.sigmoid(z))
                 for y, z in zip((y0, y1, y2, y3), (gate_z[:, n * BRANCH_WIDTH:(n + 1) * BRANCH_WIDTH] for n in range(N_BRANCHES))))


def _deepnorm_fn(x, out, g, b):
    h = DEEPNORM_ALPHA * x + out
    mu = jnp.mean(h, axis=-1, keepdims=True)
    var = jnp.mean(jnp.square(h - mu), axis=-1, keepdims=True)
    return ((h - mu) * lax.rsqrt(var + LN_EPS) * g + b,)


def _adamw_fn(g, w, m, v):
    m = ADAM_B1 * m + (1.0 - ADAM_B1) * g
    v = ADAM_B2 * v + (1.0 - ADAM_B2) * jnp.square(g)
    m_hat = m / (1.0 - ADAM_B1 ** ADAM_STEP)
    v_hat = v / (1.0 - ADAM_B2 ** ADAM_STEP)
    delta = -ADAM_LR * (m_hat / (jnp.sqrt(v_hat) + ADAM_EPS) + ADAM_WD * w)
    return delta, m, v


def split_cols(h, sizes):
    offs = np.cumsum((0,) + tuple(sizes))

    @jax.custom_vjp
    def op(h):
        return tuple(h[:, offs[i]:offs[i + 1]] for i in range(len(sizes)))

    def fwd(h):
        return op(h), None

    def bwd(_, cts):
        pad = h.shape[1] - int(offs[-1])
        parts = list(cts) + ([jnp.zeros((h.shape[0], pad), h.dtype)] if pad else [])
        return (jnp.concatenate(parts, axis=1),)

    op.defvjp(fwd, bwd)
    return op(h)


MERGE_ROWS = 256


def _merge_fwd_call(r, w_up, ps, name):
    n, rank = r.shape
    d = ps[0].shape[1]
    nb = len(ps)
    ts = min(MERGE_ROWS, n)

    def body(r_ref, w_ref, *refs):
        rb = r_ref[...].astype(MXU_DTYPE)
        acc = jnp.zeros((ts, d), F32)
        for i in range(nb):
            gate = jax.nn.sigmoid(_dot(rb, w_ref[:, i * d:(i + 1) * d]))
            acc = acc + gate * refs[i][...]
        refs[nb][...] = acc

    return pl.pallas_call(
        body, name=name,
        out_shape=jax.ShapeDtypeStruct((n, d), F32),
        grid=(n // ts,),
        in_specs=[_row_spec(ts, rank), _full_spec(w_up.shape)] + [_row_spec(ts, d)] * nb,
        out_specs=_row_spec(ts, d),
        compiler_params=_cparams(("parallel",), 2 * 4 * ts * d * (nb + 2)),
    )(r, w_up, *ps)


def _merge_bwd_call(r, w_up, ps, dm, name):
    n, rank = r.shape
    d = ps[0].shape[1]
    nb = len(ps)
    ts = min(MERGE_ROWS, n)
    steps = n // ts

    def body(r_ref, w_ref, *refs):
        p_refs, dm_ref = refs[:nb], refs[nb]
        dr_ref, dw_ref, dp_refs, dw_acc = refs[nb + 1], refs[nb + 2], refs[nb + 3:2 * nb + 3], refs[2 * nb + 3]
        step = pl.program_id(0)

        @pl.when(step == 0)
        def _():
            dw_acc[...] = jnp.zeros_like(dw_acc)

        rb = r_ref[...].astype(MXU_DTYPE)
        dmv = dm_ref[...]
        dr = jnp.zeros((ts, rank), F32)
        for i in range(nb):
            w_i = w_ref[:, i * d:(i + 1) * d]
            gate = jax.nn.sigmoid(_dot(rb, w_i))
            dp_refs[i][...] = dmv * gate
            d_logit = (dmv * p_refs[i][...] * gate * (1.0 - gate)).astype(MXU_DTYPE)
            dr = dr + _dot(d_logit, w_i, 1, 1)
            dw_acc[:, i * d:(i + 1) * d] += _dot(rb, d_logit, 0, 0)
        dr_ref[...] = dr

        @pl.when(step == steps - 1)
        def _():
            dw_ref[...] = dw_acc[...].astype(dw_ref.dtype)

    outs = pl.pallas_call(
        body, name=name,
        out_shape=[jax.ShapeDtypeStruct((n, rank), F32), jax.ShapeDtypeStruct(w_up.shape, w_up.dtype)]
        + [jax.ShapeDtypeStruct((n, d), F32)] * nb,
        grid=(steps,),
        in_specs=[_row_spec(ts, rank), _full_spec(w_up.shape)] + [_row_spec(ts, d)] * (nb + 1),
        out_specs=[_row_spec(ts, rank), _full_spec(w_up.shape)] + [_row_spec(ts, d)] * nb,
        scratch_shapes=[pltpu.VMEM(w_up.shape, F32)],
        compiler_params=_cparams(("arbitrary",), 2 * 4 * ts * d * (2 * nb + 2) + 8 * w_up.size),
    )(r, w_up, *ps, dm)
    return outs[0], outs[1], outs[2:]


def merge_gate(r, w_up, ps, name):
    @jax.custom_vjp
    def op(r, w_up, ps):
        return _merge_fwd_call(r, w_up, ps, name)

    def fwd(r, w_up, ps):
        return op(r, w_up, ps), (r, w_up, ps)

    def bwd(res, dm):
        r, w_up, ps = res
        dr, dw, dps = _merge_bwd_call(r, w_up, ps, dm, name + "_bwd")
        return dr, dw, tuple(dps)

    op.defvjp(fwd, bwd)
    return op(r, w_up, tuple(ps))


LOG2E = 1.4426950408889634


def _to_lanes(t):
    return t.transpose(0, 2, 1)


def _key_tiles(t, tk):
    h, s, d = t.shape
    return t.reshape(h, s // tk, tk, d).transpose(0, 1, 3, 2)


def _attn_vmem(sk, dq, dv, tq, tk, backward):
    resident = 2 * 2 * sk * (dq + dv) * (2 if backward else 1)
    grads = 2 * 4 * sk * (dq + dv) if backward else 0
    return resident + grads + 24 * tq * tk * 4


def _tile_ids(q0, k0, tq, tk):
    key = k0 + lax.broadcasted_iota(jnp.int32, (tk, tq), 0)
    qry = q0 + lax.broadcasted_iota(jnp.int32, (tk, tq), 1)
    return key, qry


def _lane_spec(d, tq):
    return pl.BlockSpec((1, d, tq), lambda hh, i: (hh, 0, i))


def _row_major_spec(tq, d):
    return pl.BlockSpec((1, tq, d), lambda hh, i: (hh, i, 0))


def _resident_spec(shape):
    return pl.BlockSpec((1,) + shape, lambda hh, i: (hh,) + (0,) * len(shape))


def _softmax_fwd_call(qt, k, vt, causal, name):
    h, dq, s = qt.shape
    sk, (n_kb, dv, tk) = k.shape[1], vt.shape[1:]
    tq = min(ATTN_TILE, s)
    assert not causal or (tq == tk and s == sk)

    def body(q_ref, k_ref, v_ref, o_ref, lse_ref):
        qi = pl.program_id(1)
        q_t = q_ref[0]

        def tile(kb, carry, masked):
            m, l, acc = carry
            off = pl.multiple_of(kb * tk, tk)
            st = _dot(k_ref[0, pl.ds(off, tk), :], q_t)
            if masked:
                key, qry = _tile_ids(qi * tq, kb * tk, tq, tk)
                st = jnp.where(key <= qry, st, -jnp.inf)
            m_new = jnp.maximum(m, jnp.max(st, axis=0, keepdims=True))
            alpha = jnp.exp2(m - m_new)
            p = jnp.exp2(st - m_new)
            l = alpha * l + jnp.sum(p, axis=0, keepdims=True)
            acc = alpha * acc + _dot(v_ref[0, kb], p.astype(MXU_DTYPE))
            return m_new, l, acc

        carry = (jnp.full((1, tq), -jnp.inf, F32), jnp.zeros((1, tq), F32), jnp.zeros((dv, tq), F32))
        carry = lax.fori_loop(0, qi if causal else n_kb, lambda kb, c: tile(kb, c, False), carry)
        if causal:
            carry = tile(qi, carry, True)
        m, l, acc = carry
        o_ref[0] = acc / l
        lse_ref[0] = m + jnp.log2(l)

    return pl.pallas_call(
        body, name=name,
        out_shape=[jax.ShapeDtypeStruct((h, dv, s), F32), jax.ShapeDtypeStruct((h, 1, s), F32)],
        grid=(h, s // tq),
        in_specs=[_lane_spec(dq, tq), _resident_spec((sk, dq)), _resident_spec((n_kb, dv, tk))],
        out_specs=[_lane_spec(dv, tq), _lane_spec(1, tq)],
        compiler_params=_cparams(("parallel", "arbitrary"), _attn_vmem(sk, dq, dv, tq, tk, False)),
    )(qt, k, vt)


def _softmax_bwd_call(qt, q, k, kt, v, ot, lse, dot, do, causal, name):
    h, dq, s = qt.shape
    sk, dv = k.shape[1], v.shape[2]
    n_kb, tk = kt.shape[1], kt.shape[3]
    tq = min(ATTN_TILE, s)

    def body(qt_ref, q_ref, k_ref, kt_ref, v_ref, ot_ref, lse_ref, dot_ref, do_ref, gq_ref, gk_ref, gv_ref):
        qi = pl.program_id(1)

        @pl.when(qi == 0)
        def _():
            gk_ref[...] = jnp.zeros_like(gk_ref)
            gv_ref[...] = jnp.zeros_like(gv_ref)

        q_t, q_n, do_n = qt_ref[0], q_ref[0], do_ref[0]
        do_t = dot_ref[0]
        do_tb = do_t.astype(MXU_DTYPE)
        delta = jnp.sum(do_t * ot_ref[0], axis=0, keepdims=True)
        lse_q = lse_ref[0]

        def tile(kb, gq_acc, masked):
            off = pl.multiple_of(kb * tk, tk)
            st = _dot(k_ref[0, pl.ds(off, tk), :], q_t)
            if masked:
                key, qry = _tile_ids(qi * tq, kb * tk, tq, tk)
                st = jnp.where(key <= qry, st, -jnp.inf)
            p = jnp.exp2(st - lse_q)
            dp = _dot(v_ref[0, pl.ds(off, tk), :], do_tb)
            ds = (p * (dp - delta)).astype(MXU_DTYPE)
            gk_ref[0, pl.ds(off, tk), :] += _dot(ds, q_n)
            gv_ref[0, pl.ds(off, tk), :] += _dot(p.astype(MXU_DTYPE), do_n)
            return gq_acc + _dot(kt_ref[0, kb], ds)

        gq = lax.fori_loop(0, qi if causal else n_kb, lambda kb, c: tile(kb, c, False), jnp.zeros((dq, tq), F32))
        if causal:
            gq = tile(qi, gq, True)
        gq_ref[0] = gq

    return pl.pallas_call(
        body, name=name,
        out_shape=[jax.ShapeDtypeStruct((h, dq, s), F32), jax.ShapeDtypeStruct((h, sk, dq), F32), jax.ShapeDtypeStruct((h, sk, dv), F32)],
        grid=(h, s // tq),
        in_specs=[_lane_spec(dq, tq), _row_major_spec(tq, dq), _resident_spec((sk, dq)), _resident_spec((n_kb, dq, tk)),
                  _resident_spec((sk, dv)), _lane_spec(dv, tq), _lane_spec(1, tq), _lane_spec(dv, tq), _row_major_spec(tq, dv)],
        out_specs=[_lane_spec(dq, tq), _resident_spec((sk, dq)), _resident_spec((sk, dv))],
        compiler_params=_cparams(("parallel", "arbitrary"), _attn_vmem(sk, dq, dv, tq, tk, True)),
    )(qt, q, k, kt, v, ot, lse, dot, do)


def _split_terms(x, n):
    info = jnp.finfo(MXU_DTYPE)
    terms = []
    for _ in range(n):
        t = lax.reduce_precision(x, info.nexp, info.nmant)
        terms.append(t)
        x = x - t
    return terms


def softmax_attention(q, k, v, c, scale, causal, name):
    decay = c is not None
    d = q.shape[2]
    tk = min(ATTN_TILE, k.shape[1])

    def widen(q, k, c):
        qs = q * (scale * LOG2E)
        if decay:
            terms = jnp.stack(_split_terms(c * LOG2E, 3), axis=-1)
            ones = jnp.ones_like(terms)
            pad = jnp.zeros(q.shape[:2] + (LANES - d - 6,), F32)
            qs = jnp.concatenate([qs, terms, ones, pad], axis=-1)
            k = jnp.concatenate([k, ones, -terms, pad], axis=-1)
        return qs.astype(MXU_DTYPE), k.astype(MXU_DTYPE)

    def run(q, k, v, c):
        qa, ka = widen(q, k, c)
        vb = v.astype(MXU_DTYPE)
        qt = _to_lanes(qa)
        ot, lse = _softmax_fwd_call(qt, ka, _key_tiles(vb, tk), causal, name)
        return _to_lanes(ot), (qt, qa, ka, vb, ot, lse)

    @jax.custom_vjp
    def op(q, k, v, c):
        return run(q, k, v, c)[0]

    def bwd(res, do):
        qt, qa, ka, vb, ot, lse = res
        gqt, gk, gv = _softmax_bwd_call(qt, qa, ka, _key_tiles(ka, tk), vb, ot, lse, _to_lanes(do),
                                        do.astype(MXU_DTYPE), causal, name + "_bwd")
        gq = _to_lanes(gqt)
        dc = gq[..., d] - gk[..., d + 3] if decay else None
        return gq[..., :d] * scale, gk[..., :d] * (1.0 / LOG2E), gv, dc

    op.defvjp(run, bwd)
    return op(q, k, v, c)


def _running_sum(ones, x, suffix):
    xb = x.astype(MXU_DTYPE)
    b = ones.shape[0]
    n = x.shape[0] // b
    out, carry = [None] * n, None
    for i in (reversed(range(n)) if suffix else range(n)):
        blk = _dot(ones, xb[i * b:(i + 1) * b])
        blk = blk if carry is None else blk + carry
        carry = blk[0:1, :] if suffix else blk[b - 1:b, :]
        out[i] = blk
    return out[0] if n == 1 else jnp.concatenate(out, axis=0)


def _ones_after(tk):
    b = min(tk, SUFFIX_BLOCK)
    return (lax.broadcasted_iota(jnp.int32, (b, b), 1) >= lax.broadcasted_iota(jnp.int32, (b, b), 0)).astype(MXU_DTYPE)


def _ones_before(tk):
    b = min(tk, SUFFIX_BLOCK)
    return (lax.broadcasted_iota(jnp.int32, (b, b), 1) <= lax.broadcasted_iota(jnp.int32, (b, b), 0)).astype(MXU_DTYPE)


LOGIT_CLAMP = 120.0


def _log_keep(k_tile, q_t, strict):
    z = jnp.minimum(_dot(k_tile, q_t), LOGIT_CLAMP)
    log_keep = -(jnp.log(1.0 + jnp.exp2(z)) * LOG2E)
    return z, (log_keep if strict is None else jnp.where(strict, log_keep, 0.0))


def _stick_weights(z, log_keep, later, ones_after, strict):
    local = _running_sum(ones_after, log_keep, True)
    a = jnp.exp2(z + local + later)
    return (a if strict is None else jnp.where(strict, a, 0.0)), local


def _strict_mask(qi, kb, t):
    key, qry = _tile_ids(qi * t, kb * t, t, t)
    return key < qry


def _stick_fwd_call(qt, k, vt, name):
    h, d, s = qt.shape
    t = vt.shape[3]
    n_t = s // t
    assert t == min(ATTN_TILE, s)

    def body(q_ref, k_ref, v_ref, o_ref, later_ref):
        qi = pl.program_id(1)
        q_t = q_ref[0]
        ones_after = _ones_after(t)
        later_ref[...] = jnp.zeros_like(later_ref)

        def tile(kb, carry, strict):
            later, acc = carry
            off = pl.multiple_of(kb * t, t)
            z, log_keep = _log_keep(k_ref[0, pl.ds(off, t), :], q_t, strict)
            a, local = _stick_weights(z, log_keep, later, ones_after, strict)
            later_ref[0, kb] = later
            return later + local[0:1, :], acc + _dot(v_ref[0, kb], a.astype(MXU_DTYPE))

        carry = tile(qi, (jnp.zeros((1, t), F32), jnp.zeros((d, t), F32)), _strict_mask(qi, qi, t))
        _, acc = lax.fori_loop(0, qi, lambda i, c: tile(qi - 1 - i, c, None), carry)
        o_ref[0] = acc

    later_spec = pl.BlockSpec((1, n_t, 1, t), lambda hh, i: (hh, 0, 0, i))
    return pl.pallas_call(
        body, name=name,
        out_shape=[jax.ShapeDtypeStruct((h, d, s), F32), jax.ShapeDtypeStruct((h, n_t, 1, s), F32)],
        grid=(h, n_t),
        in_specs=[_lane_spec(d, t), _resident_spec((s, d)), _resident_spec((n_t, d, t))],
        out_specs=[_lane_spec(d, t), later_spec],
        compiler_params=_cparams(("parallel", "arbitrary"), _attn_vmem(s, d, d, t, t, False)),
    )(qt, k, vt)


def _stick_bwd_call(qt, q, k, kt, v, later, dot, do, name):
    h, d, s = qt.shape
    t = kt.shape[3]
    n_t = s // t

    def body(qt_ref, q_ref, k_ref, kt_ref, v_ref, later_ref, dot_ref, do_ref, gq_ref, gk_ref, gv_ref):
        qi = pl.program_id(1)

        @pl.when(qi == 0)
        def _():
            gk_ref[...] = jnp.zeros_like(gk_ref)
            gv_ref[...] = jnp.zeros_like(gv_ref)

        q_t, q_n, do_n = qt_ref[0], q_ref[0], do_ref[0]
        do_tb = dot_ref[0].astype(MXU_DTYPE)
        ones_after, ones_before = _ones_after(t), _ones_before(t)
        diagonal = _strict_mask(qi, qi, t)

        def sweep_right(kb, carry, strict):
            before, gq_acc = carry
            off = pl.multiple_of(kb * t, t)
            z, log_keep = _log_keep(k_ref[0, pl.ds(off, t), :], q_t, strict)
            a, _ = _stick_weights(z, log_keep, later_ref[0, kb], ones_after, strict)
            g = a * _dot(v_ref[0, pl.ds(off, t), :], do_tb)
            g_prefix = _running_sum(ones_before, g, False) + before
            sig = 1.0 - jnp.exp2(log_keep)
            dz = g - sig * g_prefix
            dz = (dz if strict is None else jnp.where(strict, dz, 0.0)).astype(MXU_DTYPE)
            gk_ref[0, pl.ds(off, t), :] += _dot(dz, q_n)
            gv_ref[0, pl.ds(off, t), :] += _dot(a.astype(MXU_DTYPE), do_n)
            return g_prefix[t - 1:t, :], gq_acc + _dot(kt_ref[0, kb], dz)

        carry = lax.fori_loop(0, qi, lambda kb, c: sweep_right(kb, c, None), (jnp.zeros((1, t), F32), jnp.zeros((d, t), F32)))
        _, gq = sweep_right(qi, carry, diagonal)
        gq_ref[0] = gq

    return pl.pallas_call(
        body, name=name,
        out_shape=[jax.ShapeDtypeStruct((h, d, s), F32), jax.ShapeDtypeStruct((h, s, d), F32), jax.ShapeDtypeStruct((h, s, d), F32)],
        grid=(h, n_t),
        in_specs=[_lane_spec(d, t), _row_major_spec(t, d), _resident_spec((s, d)), _resident_spec((n_t, d, t)),
                  _resident_spec((s, d)), pl.BlockSpec((1, n_t, 1, t), lambda hh, i: (hh, 0, 0, i)),
                  _lane_spec(d, t), _row_major_spec(t, d)],
        out_specs=[_lane_spec(d, t), _resident_spec((s, d)), _resident_spec((s, d))],
        compiler_params=_cparams(("parallel", "arbitrary"), _attn_vmem(s, d, d, t, t, True)),
    )(qt, q, k, kt, v, later, dot, do)


def stick_breaking_attention(q, k, v, scale, name):
    t = min(ATTN_TILE, q.shape[1])

    def run(q, k, v):
        qb, kb, vb = (q * (scale * LOG2E)).astype(MXU_DTYPE), k.astype(MXU_DTYPE), v.astype(MXU_DTYPE)
        qt = _to_lanes(qb)
        ot, later = _stick_fwd_call(qt, kb, _key_tiles(vb, t), name)
        return _to_lanes(ot), (qt, qb, kb, vb, later)

    @jax.custom_vjp
    def op(q, k, v):
        return run(q, k, v)[0]

    def bwd(res, do):
        qt, qb, kb, vb, later = res
        gqt, gk, gv = _stick_bwd_call(qt, qb, kb, _key_tiles(kb, t), vb, later, _to_lanes(do), do.astype(MXU_DTYPE), name + "_bwd")
        return _to_lanes(gqt) * scale, gk * (1.0 / LOG2E), gv

    op.defvjp(run, bwd)
    return op(q, k, v)


def _cumsum_call(x, reverse, name):
    r, s = x.shape
    tb = min(ATTN_TILE, s)
    nb = s // tb

    def body(x_ref, o_ref, carry):
        @pl.when(pl.program_id(0) == 0)
        def _():
            carry[...] = jnp.zeros_like(carry)

        j = lax.broadcasted_iota(jnp.int32, (tb, tb), 0)
        t = lax.broadcasted_iota(jnp.int32, (tb, tb), 1)
        ones = ((j >= t) if reverse else (j <= t)).astype(MXU_DTYPE)
        rest = x_ref[...]
        y = carry[:, 0:1]
        for _ in range(3):
            term = rest.astype(MXU_DTYPE)
            rest = rest - term.astype(F32)
            y = y + _dot(term, ones)
        o_ref[...] = y
        carry[...] = jnp.broadcast_to(y[:, 0:1] if reverse else y[:, tb - 1:tb], carry.shape)

    spec = pl.BlockSpec((r, tb), (lambda i: (0, nb - 1 - i)) if reverse else (lambda i: (0, i)))
    return pl.pallas_call(
        body, name=name,
        out_shape=jax.ShapeDtypeStruct((r, s), F32),
        grid=(nb,),
        in_specs=[spec], out_specs=spec,
        scratch_shapes=[pltpu.VMEM((r, LANES), F32)],
        compiler_params=_cparams(("arbitrary",), 1024 * 1024),
    )(x)


def cumsum_lanes(x, name):
    @jax.custom_vjp
    def op(x):
        return _cumsum_call(x, False, name)

    op.defvjp(lambda x: (op(x), None), lambda _, g: (_cumsum_call(g, True, name + "_bwd"),))
    return op(x)


def _loss_call(y, target):
    n, d = y.shape
    ts = _row_tile(n, 4 * 3 * d)

    def body(y_ref, t_ref, dy_ref, loss_ref):
        err = y_ref[...] - t_ref[...]
        dy_ref[...] = err * (1.0 / d)
        part = jnp.sum(jnp.sum(err * err, axis=1, keepdims=True), axis=0, keepdims=True) * (0.5 / d)

        @pl.when(pl.program_id(0) == 0)
        def _():
            loss_ref[...] = jnp.zeros_like(loss_ref)

        loss_ref[...] += jnp.broadcast_to(part, loss_ref.shape)

    dy, loss = pl.pallas_call(
        body, name="loss_head",
        out_shape=[jax.ShapeDtypeStruct((n, d), F32), jax.ShapeDtypeStruct((SUBLANES, LANES), F32)],
        grid=(n // ts,),
        in_specs=[_row_spec(ts, d), _row_spec(ts, d)],
        out_specs=[_row_spec(ts, d), _full_spec((SUBLANES, LANES))],
        compiler_params=_cparams(("arbitrary",), 2 * ts * 4 * 3 * d),
    )(y, target)
    return loss[0, 0], dy


_FLIPS = [(0, 0, 1), (0, 1, 0), (0, 1, 1), (1, 0, 0), (1, 0, 1), (1, 1, 0), (1, 1, 1)]
MESH = pl.DeviceIdType.MESH
ANY_SPEC = pl.BlockSpec(memory_space=pl.ANY)


def _me_and_peers():
    x, y, c = lax.axis_index("x"), lax.axis_index("y"), lax.axis_index("c")
    peers = [((1 - x) if fx else x, (1 - y) if fy else y, (1 - c) if fc else c) for fx, fy, fc in _FLIPS]
    return 4 * x + 2 * y + c, peers, [4 * px + 2 * py + pc for px, py, pc in peers]


def all_gather_hbm(shard, name):
    def body(x_ref, out_ref, send_sems, recv_sems, local_sem):
        x, y, c = lax.axis_index("x"), lax.axis_index("y"), lax.axis_index("c")
        me, sibling = (x, y, c), (x, y, 1 - c)
        chips = [(1 - x, y), (x, 1 - y), (1 - x, 1 - y)]

        def block(px, py, pc):
            return out_ref.at[4 * px + 2 * py + pc]

        def copy(i, blk, to, src=None):
            return pltpu.make_async_remote_copy(src_ref=block(*blk) if src is None else src, dst_ref=block(*blk),
                                                send_sem=send_sems.at[i], recv_sem=recv_sems.at[i],
                                                device_id=to, device_id_type=MESH)

        mine = pltpu.make_async_copy(x_ref, block(*me), local_sem)
        mine.start()
        first = [copy(0, me, sibling, src=x_ref)] + [copy(1 + j, me, (*chip, c), src=x_ref) for j, chip in enumerate(chips)]
        for cp in first:
            cp.start()
        passed = [copy(4 + j, (*chip, c), sibling) for j, chip in enumerate(chips)]
        for j, chip in enumerate(chips):
            copy(1 + j, (*chip, c), me).wait_recv()
            passed[j].start()
        copy(0, sibling, me).wait_recv()
        for j, chip in enumerate(chips):
            copy(4 + j, (*chip, 1 - c), me).wait_recv()
        for cp in first + passed:
            cp.wait_send()
        mine.wait()

    return pl.pallas_call(
        body, name=name,
        out_shape=jax.ShapeDtypeStruct((N_DEV,) + shard.shape, shard.dtype),
        in_specs=[ANY_SPEC], out_specs=ANY_SPEC,
        scratch_shapes=[pltpu.SemaphoreType.DMA((N_DEV - 1,)), pltpu.SemaphoreType.DMA((N_DEV - 1,)), pltpu.SemaphoreType.DMA],
    )(shard)


def all_to_all_hbm(blocks, name):
    def body(x_ref, out_ref, send_sems, recv_sems, local_sem):
        me, peers, peer_ids = _me_and_peers()
        mine = pltpu.make_async_copy(x_ref.at[me], out_ref.at[me], local_sem)
        mine.start()
        sends = [pltpu.make_async_remote_copy(src_ref=x_ref.at[pid], dst_ref=out_ref.at[me], send_sem=send_sems.at[i],
                                              recv_sem=recv_sems.at[i], device_id=p, device_id_type=MESH)
                 for i, (p, pid) in enumerate(zip(peers, peer_ids))]
        for cp in sends:
            cp.start()
        for i, (p, pid) in enumerate(zip(peers, peer_ids)):
            pltpu.make_async_remote_copy(src_ref=x_ref.at[me], dst_ref=out_ref.at[pid], send_sem=send_sems.at[i],
                                         recv_sem=recv_sems.at[i], device_id=p, device_id_type=MESH).wait_recv()
        for cp in sends:
            cp.wait_send()
        mine.wait()

    return pl.pallas_call(
        body, name=name,
        out_shape=jax.ShapeDtypeStruct(blocks.shape, blocks.dtype),
        in_specs=[ANY_SPEC], out_specs=ANY_SPEC,
        scratch_shapes=[pltpu.SemaphoreType.DMA((N_DEV - 1,)), pltpu.SemaphoreType.DMA((N_DEV - 1,)), pltpu.SemaphoreType.DMA],
    )(blocks)


def sum_blocks(blocks, name):
    _, r, c = blocks.shape
    ts = _row_tile(r, (N_DEV * blocks.dtype.itemsize + 4) * c)

    def body(x_ref, o_ref):
        acc = x_ref[0].astype(F32)
        for d in range(1, N_DEV):
            acc = acc + x_ref[d].astype(F32)
        o_ref[...] = acc

    return pl.pallas_call(
        body, name=name,
        out_shape=jax.ShapeDtypeStruct((r, c), F32),
        grid=(r // ts,),
        in_specs=[pl.BlockSpec((N_DEV, ts, c), lambda i: (0, i, 0))],
        out_specs=_row_spec(ts, c),
        compiler_params=_cparams(("parallel",), 2 * ts * c * (N_DEV * blocks.dtype.itemsize + 4)),
    )(blocks)


def all_reduce_small(v, name):
    r, c = v.shape

    def body(x_ref, out_ref, gathered, send_sems, recv_sems):
        me, peers, peer_ids = _me_and_peers()
        sends = [pltpu.make_async_remote_copy(src_ref=x_ref, dst_ref=gathered.at[me], send_sem=send_sems.at[i],
                                              recv_sem=recv_sems.at[i], device_id=p, device_id_type=MESH)
                 for i, p in enumerate(peers)]
        for cp in sends:
            cp.start()
        gathered[me] = x_ref[...]
        for i, (p, pid) in enumerate(zip(peers, peer_ids)):
            pltpu.make_async_remote_copy(src_ref=x_ref, dst_ref=gathered.at[pid], send_sem=send_sems.at[i],
                                         recv_sem=recv_sems.at[i], device_id=p, device_id_type=MESH).wait_recv()
        for cp in sends:
            cp.wait_send()
        acc = gathered[0]
        for d in range(1, N_DEV):
            acc = acc + gathered[d]
        out_ref[...] = acc

    return pl.pallas_call(
        body, name=name,
        out_shape=jax.ShapeDtypeStruct((r, c), F32),
        in_specs=[pl.BlockSpec(memory_space=pltpu.VMEM)],
        out_specs=pl.BlockSpec(memory_space=pltpu.VMEM),
        scratch_shapes=[pltpu.VMEM((N_DEV, r, c), F32), pltpu.SemaphoreType.DMA((N_DEV - 1,)), pltpu.SemaphoreType.DMA((N_DEV - 1,))],
    )(v)


_SHARD_AXIS = {"w_in": 1, "mla_w_qb": 1, "mla_w_kvb": 2, "w_mem_kv": 1, "w_merge_up": 2, "w_branch": 3, "w_out": 1}
_SMALL = ("mla_q_norm", "mla_kv_norm", "fox_forget_bias", "ln_gain", "ln_bias")

def _regroup_qb(w):
    lead = w.shape[:-1]
    t = w.reshape(lead + (N_HEADS, MLA_NOPE + MLA_ROPE))
    half = MLA_ROPE // 2
    parts = (t[..., :MLA_NOPE], t[..., MLA_NOPE:MLA_NOPE + half], t[..., MLA_NOPE + half:])
    return jnp.concatenate([p.reshape(lead + (-1,)) for p in parts], axis=-1)


def _ungroup_qb(g):
    lead = g.shape[:-1]
    half = MLA_ROPE // 2
    n_nope, n_half = N_HEADS * MLA_NOPE, N_HEADS * half
    parts = (g[..., :n_nope].reshape(lead + (N_HEADS, MLA_NOPE)), g[..., n_nope:n_nope + n_half].reshape(lead + (N_HEADS, half)),
             g[..., n_nope + n_half:].reshape(lead + (N_HEADS, half)))
    return jnp.concatenate(parts, axis=-1).reshape(lead + (-1,))


def _to_wire(name, w):
    if name == "w_in":
        return jnp.pad(w, ((0, 0), (0, 0), (0, IN_WIDTH_PAD - IN_WIDTH)))
    if name == "mla_w_qb":
        return _regroup_qb(w)
    return w


def _from_wire(name, g):
    if name == "w_in":
        return g[:, :, :IN_WIDTH]
    if name == "mla_w_qb":
        return _ungroup_qb(g)
    return g


def _join(name, blocks):
    a = _SHARD_AXIS[name]
    t = jnp.moveaxis(blocks, 0, a)
    return t.reshape(t.shape[:a] + (t.shape[a] * t.shape[a + 1],) + t.shape[a + 2:])


def _cut(name, full):
    a = _SHARD_AXIS[name]
    t = full.reshape(full.shape[:a] + (N_DEV, full.shape[a] // N_DEV) + full.shape[a + 1:])
    return jnp.moveaxis(t, a, 0)


def _heads(t):
    s = t.shape[0]
    return t.reshape(s, N_HEADS, -1).transpose(1, 0, 2)


def _unheads(t):
    return t.transpose(1, 0, 2).reshape(t.shape[1], -1)


def _layer(x, mem, rope, w, p, tag):
    s = x.shape[0]
    cos, sin, cos4, sin4 = rope
    h = matmul(x, w["w_in"], tag + "in_proj")
    c_q, c_kv, k_rope, sb_qkv, fox_qkv, fox_f, mem_q, gate_z, merge_r = split_cols(h, IN_SPLITS)

    (cq_n,) = rowwise(_rms_fn, [c_q], [p["mla_q_norm"]], [MLA_Q_RANK], tag + "q_rms")
    q_all = matmul(cq_n, w["mla_w_qb"], tag + "q_up")
    n_nope, n_half = N_HEADS * MLA_NOPE, N_HEADS * MLA_ROPE // 2
    q_nope, q_x1, q_x2 = q_all[:, :n_nope], q_all[:, n_nope:n_nope + n_half], q_all[:, n_nope + n_half:]
    (ckv_n,) = rowwise(_rms_fn, [c_kv], [p["mla_kv_norm"]], [MLA_KV_RANK], tag + "kv_rms")
    kv = matmul(ckv_n, w["mla_w_kvb"], tag + "kv_up").reshape(s, N_HEADS, MLA_NOPE + HEAD_DIM)
    k_nope, v_mla = kv[..., :MLA_NOPE], kv[..., MLA_NOPE:]
    half = MLA_ROPE // 2
    q_o1, q_o2 = rowwise(_rope_fn, [q_x1, q_x2, cos4, sin4], [], [n_half, n_half], tag + "q_rope")
    k_o1, k_o2 = rowwise(_rope_fn, [k_rope[:, :half], k_rope[:, half:], cos, sin], [], [half, half], tag + "k_rope")
    zeros = jnp.zeros((s, N_HEADS, MLA_QK_PAD - MLA_NOPE - MLA_ROPE), F32)
    q_mla = jnp.concatenate([q_nope.reshape(s, N_HEADS, MLA_NOPE), q_o1.reshape(s, N_HEADS, half),
                             q_o2.reshape(s, N_HEADS, half), zeros], axis=-1).transpose(1, 0, 2)
    k_mla = jnp.concatenate([k_nope, jnp.broadcast_to(k_o1[:, None, :], (s, N_HEADS, half)),
                             jnp.broadcast_to(k_o2[:, None, :], (s, N_HEADS, half)), zeros], axis=-1).transpose(1, 0, 2)
    y_mla = softmax_attention(q_mla, k_mla, v_mla.transpose(1, 0, 2), None,
                              (MLA_NOPE + MLA_ROPE) ** -0.5, True, tag + "mla_attn")

    width = N_HEADS * HEAD_DIM
    sq, sk, sv = (_heads(sb_qkv[:, i * width:(i + 1) * width]) for i in range(3))
    y_sb = stick_breaking_attention(sq, sk, sv, HEAD_DIM ** -0.5, tag + "stick_attn")

    fq, fk, fv = (_heads(fox_qkv[:, i * width:(i + 1) * width]) for i in range(3))
    (log_f,) = rowwise(_forget_fn, [fox_f], [p["fox_forget_bias"]], [N_HEADS], tag + "forget_gate")
    log_f8 = jnp.pad(log_f.T, ((0, SUBLANES - N_HEADS), (0, 0)))
    c = cumsum_lanes(log_f8, tag + "forget_cumsum")[:N_HEADS]
    y_fox = softmax_attention(fq, fk, fv, c, HEAD_DIM ** -0.5, True, tag + "fox_attn")

    mkv = matmul(mem, w["w_mem_kv"], tag + "mem_kv")
    y_mem = softmax_attention(_heads(mem_q), _heads(mkv[:, :width]), _heads(mkv[:, width:]), None,
                              HEAD_DIM ** -0.5, False, tag + "mem_attn")

    ys = [_unheads(t) for t in (y_mla, y_sb, y_fox, y_mem)]
    yb = rowwise(_silu_gate_fn, ys + [gate_z], [], [BRANCH_WIDTH] * N_BRANCHES, tag + "silu_gate")
    proj = [matmul(yb[n], w["w_branch"][n], tag + "branch%d" % n) for n in range(N_BRANCHES)]
    merged = merge_gate(merge_r, w["w_merge_up"], proj, tag + "merge")
    out = matmul(merged, w["w_out"], tag + "out_proj")
    (xn,) = rowwise(_deepnorm_fn, [x, out], [p["ln_gain"], p["ln_bias"]], [x.shape[1]], tag + "deepnorm")
    return xn


def _adamw(g, w, m, v, name):
    shape = g.shape
    two_d = (-1, shape[-1])
    g2, w2, m2, v2 = (t.reshape(two_d) for t in (g, w, m, v))
    delta, new_m, new_v = _rowwise_fwd_call(_adamw_fn, [g2, w2, m2, v2], [], [shape[-1]] * 3, name)
    return delta.reshape(shape), new_m.reshape(shape), new_v.reshape(shape)


def kernel(x, mem, positions, w_in, mla_q_norm, mla_w_qb, mla_kv_norm, mla_w_kvb, fox_forget_bias, w_mem_kv, w_merge_up, w_branch, w_out, ln_gain, ln_bias, loss_target, m_w_in, m_mla_q_norm, m_mla_w_qb, m_mla_kv_norm, m_mla_w_kvb, m_fox_forget_bias, m_w_mem_kv, m_w_merge_up, m_w_branch, m_w_out, m_ln_gain, m_ln_bias, v_w_in, v_mla_q_norm, v_mla_w_qb, v_mla_kv_norm, v_mla_w_kvb, v_fox_forget_bias, v_w_mem_kv, v_w_merge_up, v_w_branch, v_w_out, v_ln_gain, v_ln_bias):
    weights = dict(w_in=w_in, mla_q_norm=mla_q_norm, mla_w_qb=mla_w_qb, mla_kv_norm=mla_kv_norm, mla_w_kvb=mla_w_kvb,
                   fox_forget_bias=fox_forget_bias, w_mem_kv=w_mem_kv, w_merge_up=w_merge_up, w_branch=w_branch,
                   w_out=w_out, ln_gain=ln_gain, ln_bias=ln_bias)
    m_in = dict(w_in=m_w_in, mla_q_norm=m_mla_q_norm, mla_w_qb=m_mla_w_qb, mla_kv_norm=m_mla_kv_norm, mla_w_kvb=m_mla_w_kvb,
                fox_forget_bias=m_fox_forget_bias, w_mem_kv=m_w_mem_kv, w_merge_up=m_w_merge_up, w_branch=m_w_branch,
                w_out=m_w_out, ln_gain=m_ln_gain, ln_bias=m_ln_bias)
    v_in = dict(w_in=v_w_in, mla_q_norm=v_mla_q_norm, mla_w_qb=v_mla_w_qb, mla_kv_norm=v_mla_kv_norm, mla_w_kvb=v_mla_w_kvb,
                fox_forget_bias=v_fox_forget_bias, w_mem_kv=v_w_mem_kv, w_merge_up=v_w_merge_up, w_branch=v_w_branch,
                w_out=v_w_out, ln_gain=v_ln_gain, ln_bias=v_ln_bias)
    order = list(weights)
    big = list(_SHARD_AXIS)
    n_layers = w_in.shape[0]
    x0, mem0 = x[0], mem[0]
    s = x0.shape[0]

    wire = [_to_wire(n, weights[n]).astype(WIRE_DTYPE) for n in big]
    sizes = [t.size for t in wire]
    flat = jnp.concatenate([t.reshape(-1) for t in wire]).reshape(-1, LANES)
    gathered = all_gather_hbm(flat, "gather_weights").reshape(N_DEV, -1)
    offs = np.cumsum([0] + sizes)
    full = {n: _join(n, gathered[:, offs[i]:offs[i + 1]].reshape((N_DEV,) + wire[i].shape)) for i, n in enumerate(big)}
    per_layer_w = [{n: full[n][l] for n in big} for l in range(n_layers)]
    per_layer_p = [{n: weights[n][l][None, :] for n in _SMALL} for l in range(n_layers)]

    inv_freq = ROPE_THETA ** (-jnp.arange(0, MLA_ROPE, 2, dtype=F32) / MLA_ROPE)
    ang = positions[0].astype(F32)[:, None] * inv_freq
    cos, sin = jnp.cos(ang), jnp.sin(ang)
    rope = (cos, sin, jnp.tile(cos, (1, N_HEADS)), jnp.tile(sin, (1, N_HEADS)))

    def trunk(x, ws, ps):
        for l in range(n_layers):
            x = _layer(x, mem0, rope, ws[l], ps[l], "l%d_" % l)
        return x

    y, pullback = jax.vjp(trunk, x0, per_layer_w, per_layer_p)
    loss_part, dy = _loss_call(y, loss_target[0])
    grad_x, grad_ws, grad_ps = pullback(dy)

    cut = [_cut(n, jnp.stack([grad_ws[l][n] for l in range(n_layers)])) for n in big]
    blocks = jnp.concatenate([t.reshape(N_DEV, -1) for t in cut], axis=1).reshape(N_DEV, -1, LANES)
    received = all_to_all_hbm(blocks, "scatter_grads")
    summed = sum_blocks(received, "sum_grads").reshape(-1)
    grads = {n: _from_wire(n, summed[offs[i]:offs[i + 1]].reshape(wire[i].shape)) for i, n in enumerate(big)}

    small = [jnp.stack([grad_ps[l][n][0] for l in range(n_layers)]) for n in _SMALL]
    small_sizes = [t.size for t in small]
    packed = jnp.concatenate([t.reshape(-1) for t in small] + [loss_part.reshape(1)])
    rows = -(-packed.size // (SUBLANES * LANES)) * SUBLANES
    packed = jnp.pad(packed, (0, rows * LANES - packed.size)).reshape(rows, LANES)
    reduced = all_reduce_small(packed, "reduce_small").reshape(-1)
    small_offs = np.cumsum([0] + small_sizes)
    for i, n in enumerate(_SMALL):
        grads[n] = reduced[small_offs[i]:small_offs[i + 1]].reshape(small[i].shape)
    loss = reduced[small_offs[-1]]

    delta, new_m, new_v = {}, {}, {}
    for n in order:
        delta[n], new_m[n], new_v[n] = _adamw(grads[n], weights[n], m_in[n], v_in[n], "adamw_" + n)
    return (loss, grad_x[None], *[grads[n] for n in order], *[delta[n] for n in order],
            *[new_m[n] for n in order], *[new_v[n] for n in order])
```

```python
import functools

import numpy as np
import jax
import jax.numpy as jnp
from jax import lax
from jax.experimental import pallas as pl
from jax.experimental.pallas import tpu as pltpu

F32 = jnp.float32
MXU_DTYPE = jnp.bfloat16
WIRE_DTYPE = jnp.bfloat16

N_LAYERS = 4
N_BRANCHES = 4
BRANCH_WIDTH = 256
N_HEADS = 4
HEAD_DIM = 64
MLA_NOPE = 64
MLA_ROPE = 32
MLA_Q_RANK = 384
MLA_KV_RANK = 256
ROPE_THETA = 10000.0
MERGE_RANK = 128
RMS_EPS = 1e-6
LN_EPS = 1e-5
DEEPNORM_ALPHA = (2 * N_LAYERS) ** 0.25
IN_SPLITS = (MLA_Q_RANK, MLA_KV_RANK, MLA_ROPE, 3 * N_HEADS * HEAD_DIM, 3 * N_HEADS * HEAD_DIM,
             N_HEADS, N_HEADS * HEAD_DIM, N_BRANCHES * BRANCH_WIDTH, MERGE_RANK)
IN_WIDTH = sum(IN_SPLITS)
ADAM_LR, ADAM_B1, ADAM_B2, ADAM_EPS, ADAM_WD, ADAM_STEP = 0.001, 0.9, 0.999, 1e-08, 0.01, 10

LANES = 128
SUBLANES = 8
VMEM_BYTES_V7X = 64 * 1024 * 1024
VMEM_LIMIT_CAP = VMEM_BYTES_V7X - 8 * 1024 * 1024
MM_VMEM_BUDGET = 44 * 1024 * 1024
ROW_TILE_BYTES = 8 * 1024 * 1024
ATTN_TILE = 512
SUFFIX_BLOCK = 256
N_DEV = 8
IN_WIDTH_PAD = -(-IN_WIDTH // LANES) * LANES
MLA_QK_PAD = 128


def _cparams(semantics, vmem_estimate):
    limit = int(min(max(2 * vmem_estimate, 32 * 1024 * 1024), VMEM_LIMIT_CAP))
    return pltpu.CompilerParams(dimension_semantics=semantics, vmem_limit_bytes=limit)


def _dot(a, b, ca=1, cb=0):
    return lax.dot_general(a, b, (((ca,), (cb,)), ((), ())), preferred_element_type=F32)


def _div_tile(n, cap, align):
    t = (cap // align) * align
    while t >= align:
        if n % t == 0:
            return t
        t -= align
    return n


def _mm_call(a, b, ta, tb, out_dtype, name):
    k, m = a.shape if ta else a.shape[::-1]
    n, kb = b.shape if tb else b.shape[::-1]
    assert k == kb, (a.shape, b.shape, ta, tb)
    ab, bb, ob = a.dtype.itemsize, b.dtype.itemsize, jnp.dtype(out_dtype).itemsize
    tk = k if k <= 4096 else 512
    tn = n if n <= 4096 else _div_tile(n, 2048, LANES)
    cands = [t for t in (1024, 512, 256, 128) if t <= m and m % t == 0] or [m]
    for tm in cands:
        est = 2 * (tm * tk * ab + tk * tn * bb + tm * tn * ob) + (tm * tn * 4 if k > tk else 0)
        if est <= MM_VMEM_BUDGET:
            break
    nk = k // tk

    def body(a_ref, b_ref, o_ref, *scratch):
        p = _dot(a_ref[...].astype(MXU_DTYPE), b_ref[...].astype(MXU_DTYPE), 0 if ta else 1, 1 if tb else 0)
        if nk == 1:
            o_ref[...] = p.astype(o_ref.dtype)
        else:
            acc = scratch[0]
            kk = pl.program_id(2)

            @pl.when(kk == 0)
            def _():
                acc[...] = p

            @pl.when(kk > 0)
            def _():
                acc[...] += p

            @pl.when(kk == nk - 1)
            def _():
                o_ref[...] = acc[...].astype(o_ref.dtype)

    a_spec = pl.BlockSpec((tk, tm), lambda i, j, l: (l, i)) if ta else pl.BlockSpec((tm, tk), lambda i, j, l: (i, l))
    b_spec = pl.BlockSpec((tn, tk), lambda i, j, l: (j, l)) if tb else pl.BlockSpec((tk, tn), lambda i, j, l: (l, j))
    return pl.pallas_call(
        body, name=name,
        out_shape=jax.ShapeDtypeStruct((m, n), out_dtype),
        grid=(m // tm, n // tn, nk),
        in_specs=[a_spec, b_spec],
        out_specs=pl.BlockSpec((tm, tn), lambda i, j, l: (i, j)),
        scratch_shapes=[pltpu.VMEM((tm, tn), F32)] if nk > 1 else [],
        compiler_params=_cparams(("parallel", "parallel", "arbitrary"), est),
    )(a, b)


def matmul(a, b, name):
    @jax.custom_vjp
    def op(a, b):
        return _mm_call(a, b, False, False, F32, name)

    def fwd(a, b):
        return op(a, b), (a, b)

    def bwd(res, g):
        a, b = res
        da = _mm_call(g, b, False, True, a.dtype, name + "_da")
        db = _mm_call(a, g, True, False, b.dtype, name + "_db")
        return da, db

    op.defvjp(fwd, bwd)
    return op(a, b)


def _row_tile(n_rows, bytes_per_row):
    ts = 1024
    while ts > SUBLANES and (ts * bytes_per_row > ROW_TILE_BYTES or n_rows % ts):
        ts //= 2
    return ts if n_rows % ts == 0 else n_rows


def _row_spec(ts, cols):
    return pl.BlockSpec((ts, cols), lambda i: (i, 0))


def _full_spec(shape):
    return pl.BlockSpec(shape, lambda i: (0,) * len(shape))


def _rowwise_fwd_call(f, rows, params, out_cols, name):
    n = rows[0].shape[0]
    per_row = 4 * (sum(r.shape[1] for r in rows) + sum(out_cols))
    ts = _row_tile(n, per_row)
    n_in, n_p = len(rows), len(params)

    def body(*refs):
        ins = [r[...] for r in refs[:n_in + n_p]]
        outs = f(*ins)
        for o_ref, o in zip(refs[n_in + n_p:], outs):
            o_ref[...] = o.astype(o_ref.dtype)

    return pl.pallas_call(
        body, name=name,
        out_shape=[jax.ShapeDtypeStruct((n, c), F32) for c in out_cols],
        grid=(n // ts,),
        in_specs=[_row_spec(ts, r.shape[1]) for r in rows] + [_full_spec(p.shape) for p in params],
        out_specs=[_row_spec(ts, c) for c in out_cols],
        compiler_params=_cparams(("parallel",), 2 * ts * per_row),
    )(*rows, *params)


def _rowwise_bwd_call(f, rows, params, cts, name):
    n = rows[0].shape[0]
    per_row = 4 * (2 * sum(r.shape[1] for r in rows) + 2 * sum(c.shape[1] for c in cts))
    ts = _row_tile(n, per_row)
    n_in, n_p, n_ct = len(rows), len(params), len(cts)

    def body(*refs):
        ins = [r[...] for r in refs[:n_in + n_p]]
        ct = tuple(r[...] for r in refs[n_in + n_p:n_in + n_p + n_ct])
        grads = jax.vjp(lambda *a: tuple(f(*a)), *ins)[1](ct)
        out_refs = refs[n_in + n_p + n_ct:]
        for o_ref, g in zip(out_refs[:n_in], grads[:n_in]):
            o_ref[...] = g
        first = pl.program_id(0) == 0
        for o_ref, g in zip(out_refs[n_in:], grads[n_in:]):
            @pl.when(first)
            def _(o_ref=o_ref, g=g):
                o_ref[...] = g

            @pl.when(jnp.logical_not(first))
            def _(o_ref=o_ref, g=g):
                o_ref[...] += g

    outs = pl.pallas_call(
        body, name=name,
        out_shape=[jax.ShapeDtypeStruct(r.shape, F32) for r in rows] + [jax.ShapeDtypeStruct(p.shape, F32) for p in params],
        grid=(n // ts,),
        in_specs=([_row_spec(ts, r.shape[1]) for r in rows] + [_full_spec(p.shape) for p in params]
                  + [_row_spec(ts, c.shape[1]) for c in cts]),
        out_specs=[_row_spec(ts, r.shape[1]) for r in rows] + [_full_spec(p.shape) for p in params],
        compiler_params=_cparams(("arbitrary",), 2 * ts * per_row),
    )(*rows, *params, *cts)
    return outs[:n_in], outs[n_in:]


def rowwise(f, rows, params, out_cols, name):
    @jax.custom_vjp
    def op(rows, params):
        return tuple(_rowwise_fwd_call(f, rows, params, out_cols, name))

    def fwd(rows, params):
        return op(rows, params), (rows, params)

    def bwd(res, cts):
        rows, params = res
        d_rows, d_params = _rowwise_bwd_call(f, rows, params, cts, name + "_bwd")
        return tuple(d_rows), tuple(d_params)

    op.defvjp(fwd, bwd)
    return op(tuple(rows), tuple(params))


def _rms_fn(x, g):
    return (x * lax.rsqrt(jnp.mean(x * x, axis=-1, keepdims=True) + RMS_EPS) * g,)


def _rope_fn(x1, x2, cos, sin):
    return x1 * cos - x2 * sin, x1 * sin + x2 * cos


def _log_sigmoid(x):
    return jnp.minimum(x, 0.0) - jnp.log1p(jnp.exp(-jnp.abs(x)))


def _forget_fn(f, bias):
    return (_log_sigmoid(f + bias),)


def _silu_gate_fn(y0, y1, y2, y3, gate_z):
    return tuple(y * (z * jax.nn.sigmoid(z))
                 for y, z in zip((y0, y1, y2, y3), (gate_z[:, n * BRANCH_WIDTH:(n + 1) * BRANCH_WIDTH] for n in range(N_BRANCHES))))


def _deepnorm_fn(x, out, g, b):
    h = DEEPNORM_ALPHA * x + out
    mu = jnp.mean(h, axis=-1, keepdims=True)
    var = jnp.mean(jnp.square(h - mu), axis=-1, keepdims=True)
    return ((h - mu) * lax.rsqrt(var + LN_EPS) * g + b,)


def _adamw_fn(g, w, m, v):
    m = ADAM_B1 * m + (1.0 - ADAM_B1) * g
    v = ADAM_B2 * v + (1.0 - ADAM_B2) * jnp.square(g)
    m_hat = m / (1.0 - ADAM_B1 ** ADAM_STEP)
    v_hat = v / (1.0 - ADAM_B2 ** ADAM_STEP)
    delta = -ADAM_LR * (m_hat / (jnp.sqrt(v_hat) + ADAM_EPS) + ADAM_WD * w)
    return delta, m, v


def in_proj(x, w, sizes, name):
    offs = np.cumsum((0,) + tuple(sizes))
    pad = w.shape[1] - int(offs[-1])

    @jax.custom_vjp
    def op(x, w):
        h = _mm_call(x, w, False, False, F32, name)
        return tuple(h[:, offs[i]:offs[i + 1]] for i in range(len(sizes)))

    def fwd(x, w):
        return op(x, w), (x, w)

    def bwd(res, cts):
        x, w = res
        parts = [c.astype(MXU_DTYPE) for c in cts] + ([jnp.zeros((x.shape[0], pad), MXU_DTYPE)] if pad else [])
        dh = jnp.concatenate(parts, axis=1)
        return _mm_call(dh, w, False, True, x.dtype, name + "_da"), _mm_call(x, dh, True, False, w.dtype, name + "_db")

    op.defvjp(fwd, bwd)
    return op(x, w)


MERGE_ROWS = 256


def _merge_fwd_call(r, w_up, ps, name):
    n, rank = r.shape
    d = ps[0].shape[1]
    nb = len(ps)
    ts = min(MERGE_ROWS, n)

    def body(r_ref, w_ref, *refs):
        rb = r_ref[...].astype(MXU_DTYPE)
        acc = jnp.zeros((ts, d), F32)
        for i in range(nb):
            gate = jax.nn.sigmoid(_dot(rb, w_ref[:, i * d:(i + 1) * d]))
            acc = acc + gate * refs[i][...]
        refs[nb][...] = acc

    return pl.pallas_call(
        body, name=name,
        out_shape=jax.ShapeDtypeStruct((n, d), F32),
        grid=(n // ts,),
        in_specs=[_row_spec(ts, rank), _full_spec(w_up.shape)] + [_row_spec(ts, d)] * nb,
        out_specs=_row_spec(ts, d),
        compiler_params=_cparams(("parallel",), 2 * 4 * ts * d * (nb + 2)),
    )(r, w_up, *ps)


def _merge_bwd_call(r, w_up, ps, dm, name):
    n, rank = r.shape
    d = ps[0].shape[1]
    nb = len(ps)
    ts = min(MERGE_ROWS, n)
    steps = n // ts

    def body(r_ref, w_ref, *refs):
        p_refs, dm_ref = refs[:nb], refs[nb]
        dr_ref, dw_ref, dp_refs, dw_acc = refs[nb + 1], refs[nb + 2], refs[nb + 3:2 * nb + 3], refs[2 * nb + 3]
        step = pl.program_id(0)

        @pl.when(step == 0)
        def _():
            dw_acc[...] = jnp.zeros_like(dw_acc)

        rb = r_ref[...].astype(MXU_DTYPE)
        dmv = dm_ref[...]
        dr = jnp.zeros((ts, rank), F32)
        for i in range(nb):
            w_i = w_ref[:, i * d:(i + 1) * d]
            gate = jax.nn.sigmoid(_dot(rb, w_i))
            dp_refs[i][...] = dmv * gate
            d_logit = (dmv * p_refs[i][...] * gate * (1.0 - gate)).astype(MXU_DTYPE)
            dr = dr + _dot(d_logit, w_i, 1, 1)
            dw_acc[:, i * d:(i + 1) * d] += _dot(rb, d_logit, 0, 0)
        dr_ref[...] = dr

        @pl.when(step == steps - 1)
        def _():
            dw_ref[...] = dw_acc[...].astype(dw_ref.dtype)

    outs = pl.pallas_call(
        body, name=name,
        out_shape=[jax.ShapeDtypeStruct((n, rank), F32), jax.ShapeDtypeStruct(w_up.shape, w_up.dtype)]
        + [jax.ShapeDtypeStruct((n, d), F32)] * nb,
        grid=(steps,),
        in_specs=[_row_spec(ts, rank), _full_spec(w_up.shape)] + [_row_spec(ts, d)] * (nb + 1),
        out_specs=[_row_spec(ts, rank), _full_spec(w_up.shape)] + [_row_spec(ts, d)] * nb,
        scratch_shapes=[pltpu.VMEM(w_up.shape, F32)],
        compiler_params=_cparams(("arbitrary",), 2 * 4 * ts * d * (2 * nb + 2) + 8 * w_up.size),
    )(r, w_up, *ps, dm)
    return outs[0], outs[1], outs[2:]


def merge_gate(r, w_up, ps, name):
    @jax.custom_vjp
    def op(r, w_up, ps):
        return _merge_fwd_call(r, w_up, ps, name)

    def fwd(r, w_up, ps):
        return op(r, w_up, ps), (r, w_up, ps)

    def bwd(res, dm):
        r, w_up, ps = res
        dr, dw, dps = _merge_bwd_call(r, w_up, ps, dm, name + "_bwd")
        return dr, dw, tuple(dps)

    op.defvjp(fwd, bwd)
    return op(r, w_up, tuple(ps))


LOG2E = 1.4426950408889634


def _to_lanes(t):
    return t.transpose(0, 2, 1)


def _key_tiles(t, tk):
    h, s, d = t.shape
    return t.reshape(h, s // tk, tk, d).transpose(0, 1, 3, 2)


def _attn_vmem(sk, dq, dv, tq, tk, backward):
    resident = 2 * 2 * sk * (dq + dv) * (2 if backward else 1)
    grads = 2 * 4 * sk * (dq + dv) if backward else 0
    return resident + grads + 24 * tq * tk * 4


def _tile_ids(q0, k0, tq, tk):
    key = k0 + lax.broadcasted_iota(jnp.int32, (tk, tq), 0)
    qry = q0 + lax.broadcasted_iota(jnp.int32, (tk, tq), 1)
    return key, qry


def _lane_spec(d, tq):
    return pl.BlockSpec((1, d, tq), lambda hh, i: (hh, 0, i))


def _row_major_spec(tq, d):
    return pl.BlockSpec((1, tq, d), lambda hh, i: (hh, i, 0))


def _resident_spec(shape):
    return pl.BlockSpec((1,) + shape, lambda hh, i: (hh,) + (0,) * len(shape))


def _softmax_fwd_call(qt, k, vt, causal, name):
    h, dq, s = qt.shape
    sk, (n_kb, dv, tk) = k.shape[1], vt.shape[1:]
    tq = min(ATTN_TILE, s)
    assert not causal or (tq == tk and s == sk)

    def body(q_ref, k_ref, v_ref, o_ref, lse_ref):
        qi = pl.program_id(1)
        q_t = q_ref[0]

        def tile(kb, carry, masked):
            m, l, acc = carry
            off = pl.multiple_of(kb * tk, tk)
            st = _dot(k_ref[0, pl.ds(off, tk), :], q_t)
            if masked:
                key, qry = _tile_ids(qi * tq, kb * tk, tq, tk)
                st = jnp.where(key <= qry, st, -jnp.inf)
            m_new = jnp.maximum(m, jnp.max(st, axis=0, keepdims=True))
            alpha = jnp.exp2(m - m_new)
            p = jnp.exp2(st - m_new)
            l = alpha * l + jnp.sum(p, axis=0, keepdims=True)
            acc = alpha * acc + _dot(v_ref[0, kb], p.astype(MXU_DTYPE))
            return m_new, l, acc

        carry = (jnp.full((1, tq), -jnp.inf, F32), jnp.zeros((1, tq), F32), jnp.zeros((dv, tq), F32))
        carry = lax.fori_loop(0, qi if causal else n_kb, lambda kb, c: tile(kb, c, False), carry)
        if causal:
            carry = tile(qi, carry, True)
        m, l, acc = carry
        o_ref[0] = acc / l
        lse_ref[0] = m + jnp.log2(l)

    return pl.pallas_call(
        body, name=name,
        out_shape=[jax.ShapeDtypeStruct((h, dv, s), F32), jax.ShapeDtypeStruct((h, 1, s), F32)],
        grid=(h, s // tq),
        in_specs=[_lane_spec(dq, tq), _resident_spec((sk, dq)), _resident_spec((n_kb, dv, tk))],
        out_specs=[_lane_spec(dv, tq), _lane_spec(1, tq)],
        compiler_params=_cparams(("parallel", "arbitrary"), _attn_vmem(sk, dq, dv, tq, tk, False)),
    )(qt, k, vt)


def _softmax_bwd_call(qt, q, k, kt, v, ot, lse, dot, do, causal, name):
    h, dq, s = qt.shape
    sk, dv = k.shape[1], v.shape[2]
    n_kb, tk = kt.shape[1], kt.shape[3]
    tq = min(ATTN_TILE, s)

    def body(qt_ref, q_ref, k_ref, kt_ref, v_ref, ot_ref, lse_ref, dot_ref, do_ref, gq_ref, gk_ref, gv_ref):
        qi = pl.program_id(1)

        @pl.when(qi == 0)
        def _():
            gk_ref[...] = jnp.zeros_like(gk_ref)
            gv_ref[...] = jnp.zeros_like(gv_ref)

        q_t, q_n, do_n = qt_ref[0], q_ref[0], do_ref[0]
        do_t = dot_ref[0]
        do_tb = do_t.astype(MXU_DTYPE)
        delta = jnp.sum(do_t * ot_ref[0], axis=0, keepdims=True)
        lse_q = lse_ref[0]

        def tile(kb, gq_acc, masked):
            off = pl.multiple_of(kb * tk, tk)
            st = _dot(k_ref[0, pl.ds(off, tk), :], q_t)
            if masked:
                key, qry = _tile_ids(qi * tq, kb * tk, tq, tk)
                st = jnp.where(key <= qry, st, -jnp.inf)
            p = jnp.exp2(st - lse_q)
            dp = _dot(v_ref[0, pl.ds(off, tk), :], do_tb)
            ds = (p * (dp - delta)).astype(MXU_DTYPE)
            gk_ref[0, pl.ds(off, tk), :] += _dot(ds, q_n)
            gv_ref[0, pl.ds(off, tk), :] += _dot(p.astype(MXU_DTYPE), do_n)
            return gq_acc + _dot(kt_ref[0, kb], ds)

        gq = lax.fori_loop(0, qi if causal else n_kb, lambda kb, c: tile(kb, c, False), jnp.zeros((dq, tq), F32))
        if causal:
            gq = tile(qi, gq, True)
        gq_ref[0] = gq

    return pl.pallas_call(
        body, name=name,
        out_shape=[jax.ShapeDtypeStruct((h, dq, s), F32), jax.ShapeDtypeStruct((h, sk, dq), F32), jax.ShapeDtypeStruct((h, sk, dv), F32)],
        grid=(h, s // tq),
        in_specs=[_lane_spec(dq, tq), _row_major_spec(tq, dq), _resident_spec((sk, dq)), _resident_spec((n_kb, dq, tk)),
                  _resident_spec((sk, dv)), _lane_spec(dv, tq), _lane_spec(1, tq), _lane_spec(dv, tq), _row_major_spec(tq, dv)],
        out_specs=[_lane_spec(dq, tq), _resident_spec((sk, dq)), _resident_spec((sk, dv))],
        compiler_params=_cparams(("parallel", "arbitrary"), _attn_vmem(sk, dq, dv, tq, tk, True)),
    )(qt, q, k, kt, v, ot, lse, dot, do)


def _split_terms(x, n):
    info = jnp.finfo(MXU_DTYPE)
    terms = []
    for _ in range(n):
        t = lax.reduce_precision(x, info.nexp, info.nmant)
        terms.append(t)
        x = x - t
    return terms


def softmax_attention(q, k, v, c, scale, causal, name):
    decay = c is not None
    d = q.shape[2]
    tk = min(ATTN_TILE, k.shape[1])

    def widen(q, k, c):
        qs = q * (scale * LOG2E)
        if decay:
            terms = jnp.stack(_split_terms(c * LOG2E, 3), axis=-1)
            ones = jnp.ones_like(terms)
            pad = jnp.zeros(q.shape[:2] + (LANES - d - 6,), F32)
            qs = jnp.concatenate([qs, terms, ones, pad], axis=-1)
            k = jnp.concatenate([k, ones, -terms, pad], axis=-1)
        return qs.astype(MXU_DTYPE), k.astype(MXU_DTYPE)

    def run(q, k, v, c):
        qa, ka = widen(q, k, c)
        vb = v.astype(MXU_DTYPE)
        qt = _to_lanes(qa)
        ot, lse = _softmax_fwd_call(qt, ka, _key_tiles(vb, tk), causal, name)
        return _to_lanes(ot), (qt, qa, ka, vb, ot, lse)

    @jax.custom_vjp
    def op(q, k, v, c):
        return run(q, k, v, c)[0]

    def bwd(res, do):
        qt, qa, ka, vb, ot, lse = res
        gqt, gk, gv = _softmax_bwd_call(qt, qa, ka, _key_tiles(ka, tk), vb, ot, lse, _to_lanes(do),
                                        do.astype(MXU_DTYPE), causal, name + "_bwd")
        gq = _to_lanes(gqt)
        dc = gq[..., d] - gk[..., d + 3] if decay else None
        return gq[..., :d] * scale, gk[..., :d] * (1.0 / LOG2E), gv, dc

    op.defvjp(run, bwd)
    return op(q, k, v, c)


def _running_sum(ones, x, suffix):
    xb = x.astype(MXU_DTYPE)
    b = ones.shape[0]
    n = x.shape[0] // b
    out, carry = [None] * n, None
    for i in (reversed(range(n)) if suffix else range(n)):
        blk = _dot(ones, xb[i * b:(i + 1) * b])
        blk = blk if carry is None else blk + carry
        carry = blk[0:1, :] if suffix else blk[b - 1:b, :]
        out[i] = blk
    return out[0] if n == 1 else jnp.concatenate(out, axis=0)


def _ones_after(tk):
    b = min(tk, SUFFIX_BLOCK)
    return (lax.broadcasted_iota(jnp.int32, (b, b), 1) >= lax.broadcasted_iota(jnp.int32, (b, b), 0)).astype(MXU_DTYPE)


def _ones_before(tk):
    b = min(tk, SUFFIX_BLOCK)
    return (lax.broadcasted_iota(jnp.int32, (b, b), 1) <= lax.broadcasted_iota(jnp.int32, (b, b), 0)).astype(MXU_DTYPE)


LOGIT_CLAMP = 120.0


def _log_keep(k_tile, q_t, strict):
    z = jnp.minimum(_dot(k_tile, q_t), LOGIT_CLAMP)
    log_keep = -(jnp.log(1.0 + jnp.exp2(z)) * LOG2E)
    return z, (log_keep if strict is None else jnp.where(strict, log_keep, 0.0))


def _stick_weights(z, log_keep, later, ones_after, strict):
    local = _running_sum(ones_after, log_keep, True)
    a = jnp.exp2(z + local + later)
    return (a if strict is None else jnp.where(strict, a, 0.0)), local


def _strict_mask(qi, kb, t):
    key, qry = _tile_ids(qi * t, kb * t, t, t)
    return key < qry


def _stick_fwd_call(qt, k, vt, name):
    h, d, s = qt.shape
    t = vt.shape[3]
    n_t = s // t
    assert t == min(ATTN_TILE, s)

    def body(q_ref, k_ref, v_ref, o_ref, later_ref):
        qi = pl.program_id(1)
        q_t = q_ref[0]
        ones_after = _ones_after(t)
        later_ref[...] = jnp.zeros_like(later_ref)

        def tile(kb, carry, strict):
            later, acc = carry
            off = pl.multiple_of(kb * t, t)
            z, log_keep = _log_keep(k_ref[0, pl.ds(off, t), :], q_t, strict)
            a, local = _stick_weights(z, log_keep, later, ones_after, strict)
            later_ref[0, kb] = later
            return later + local[0:1, :], acc + _dot(v_ref[0, kb], a.astype(MXU_DTYPE))

        carry = tile(qi, (jnp.zeros((1, t), F32), jnp.zeros((d, t), F32)), _strict_mask(qi, qi, t))
        _, acc = lax.fori_loop(0, qi, lambda i, c: tile(qi - 1 - i, c, None), carry)
        o_ref[0] = acc

    later_spec = pl.BlockSpec((1, n_t, 1, t), lambda hh, i: (hh, 0, 0, i))
    return pl.pallas_call(
        body, name=name,
        out_shape=[jax.ShapeDtypeStruct((h, d, s), F32), jax.ShapeDtypeStruct((h, n_t, 1, s), F32)],
        grid=(h, n_t),
        in_specs=[_lane_spec(d, t), _resident_spec((s, d)), _resident_spec((n_t, d, t))],
        out_specs=[_lane_spec(d, t), later_spec],
        compiler_params=_cparams(("parallel", "arbitrary"), _attn_vmem(s, d, d, t, t, False)),
    )(qt, k, vt)


def _stick_bwd_call(qt, q, k, kt, v, later, dot, do, name):
    h, d, s = qt.shape
    t = kt.shape[3]
    n_t = s // t

    def body(qt_ref, q_ref, k_ref, kt_ref, v_ref, later_ref, dot_ref, do_ref, gq_ref, gk_ref, gv_ref):
        qi = pl.program_id(1)

        @pl.when(qi == 0)
        def _():
            gk_ref[...] = jnp.zeros_like(gk_ref)
            gv_ref[...] = jnp.zeros_like(gv_ref)

        q_t, q_n, do_n = qt_ref[0], q_ref[0], do_ref[0]
        do_tb = dot_ref[0].astype(MXU_DTYPE)
        ones_after, ones_before = _ones_after(t), _ones_before(t)
        diagonal = _strict_mask(qi, qi, t)

        def sweep_right(kb, carry, strict):
            before, gq_acc = carry
            off = pl.multiple_of(kb * t, t)
            z, log_keep = _log_keep(k_ref[0, pl.ds(off, t), :], q_t, strict)
            a, _ = _stick_weights(z, log_keep, later_ref[0, kb], ones_after, strict)
            g = a * _dot(v_ref[0, pl.ds(off, t), :], do_tb)
            g_prefix = _running_sum(ones_before, g, False) + before
            sig = 1.0 - jnp.exp2(log_keep)
            dz = g - sig * g_prefix
            dz = (dz if strict is None else jnp.where(strict, dz, 0.0)).astype(MXU_DTYPE)
            gk_ref[0, pl.ds(off, t), :] += _dot(dz, q_n)
            gv_ref[0, pl.ds(off, t), :] += _dot(a.astype(MXU_DTYPE), do_n)
            return g_prefix[t - 1:t, :], gq_acc + _dot(kt_ref[0, kb], dz)

        carry = lax.fori_loop(0, qi, lambda kb, c: sweep_right(kb, c, None), (jnp.zeros((1, t), F32), jnp.zeros((d, t), F32)))
        _, gq = sweep_right(qi, carry, diagonal)
        gq_ref[0] = gq

    return pl.pallas_call(
        body, name=name,
        out_shape=[jax.ShapeDtypeStruct((h, d, s), F32), jax.ShapeDtypeStruct((h, s, d), F32), jax.ShapeDtypeStruct((h, s, d), F32)],
        grid=(h, n_t),
        in_specs=[_lane_spec(d, t), _row_major_spec(t, d), _resident_spec((s, d)), _resident_spec((n_t, d, t)),
                  _resident_spec((s, d)), pl.BlockSpec((1, n_t, 1, t), lambda hh, i: (hh, 0, 0, i)),
                  _lane_spec(d, t), _row_major_spec(t, d)],
        out_specs=[_lane_spec(d, t), _resident_spec((s, d)), _resident_spec((s, d))],
        compiler_params=_cparams(("parallel", "arbitrary"), _attn_vmem(s, d, d, t, t, True)),
    )(qt, q, k, kt, v, later, dot, do)


def stick_breaking_attention(q, k, v, scale, name):
    t = min(ATTN_TILE, q.shape[1])

    def run(q, k, v):
        qb, kb, vb = (q * (scale * LOG2E)).astype(MXU_DTYPE), k.astype(MXU_DTYPE), v.astype(MXU_DTYPE)
        qt = _to_lanes(qb)
        ot, later = _stick_fwd_call(qt, kb, _key_tiles(vb, t), name)
        return _to_lanes(ot), (qt, qb, kb, vb, later)

    @jax.custom_vjp
    def op(q, k, v):
        return run(q, k, v)[0]

    def bwd(res, do):
        qt, qb, kb, vb, later = res
        gqt, gk, gv = _stick_bwd_call(qt, qb, kb, _key_tiles(kb, t), vb, later, _to_lanes(do), do.astype(MXU_DTYPE), name + "_bwd")
        return _to_lanes(gqt) * scale, gk * (1.0 / LOG2E), gv

    op.defvjp(run, bwd)
    return op(q, k, v)


def _cumsum_call(x, reverse, name):
    r, s = x.shape
    tb = min(ATTN_TILE, s)
    nb = s // tb

    def body(x_ref, o_ref, carry):
        @pl.when(pl.program_id(0) == 0)
        def _():
            carry[...] = jnp.zeros_like(carry)

        j = lax.broadcasted_iota(jnp.int32, (tb, tb), 0)
        t = lax.broadcasted_iota(jnp.int32, (tb, tb), 1)
        ones = ((j >= t) if reverse else (j <= t)).astype(MXU_DTYPE)
        rest = x_ref[...]
        y = carry[:, 0:1]
        for _ in range(3):
            term = rest.astype(MXU_DTYPE)
            rest = rest - term.astype(F32)
            y = y + _dot(term, ones)
        o_ref[...] = y
        carry[...] = jnp.broadcast_to(y[:, 0:1] if reverse else y[:, tb - 1:tb], carry.shape)

    spec = pl.BlockSpec((r, tb), (lambda i: (0, nb - 1 - i)) if reverse else (lambda i: (0, i)))
    return pl.pallas_call(
        body, name=name,
        out_shape=jax.ShapeDtypeStruct((r, s), F32),
        grid=(nb,),
        in_specs=[spec], out_specs=spec,
        scratch_shapes=[pltpu.VMEM((r, LANES), F32)],
        compiler_params=_cparams(("arbitrary",), 1024 * 1024),
    )(x)


def cumsum_lanes(x, name):
    @jax.custom_vjp
    def op(x):
        return _cumsum_call(x, False, name)

    op.defvjp(lambda x: (op(x), None), lambda _, g: (_cumsum_call(g, True, name + "_bwd"),))
    return op(x)


def _loss_call(y, target):
    n, d = y.shape
    ts = _row_tile(n, 4 * 3 * d)

    def body(y_ref, t_ref, dy_ref, loss_ref):
        err = y_ref[...] - t_ref[...]
        dy_ref[...] = err * (1.0 / d)
        part = jnp.sum(jnp.sum(err * err, axis=1, keepdims=True), axis=0, keepdims=True) * (0.5 / d)

        @pl.when(pl.program_id(0) == 0)
        def _():
            loss_ref[...] = jnp.zeros_like(loss_ref)

        loss_ref[...] += jnp.broadcast_to(part, loss_ref.shape)

    dy, loss = pl.pallas_call(
        body, name="loss_head",
        out_shape=[jax.ShapeDtypeStruct((n, d), F32), jax.ShapeDtypeStruct((SUBLANES, LANES), F32)],
        grid=(n // ts,),
        in_specs=[_row_spec(ts, d), _row_spec(ts, d)],
        out_specs=[_row_spec(ts, d), _full_spec((SUBLANES, LANES))],
        compiler_params=_cparams(("arbitrary",), 2 * ts * 4 * 3 * d),
    )(y, target)
    return loss[0, 0], dy


_FLIPS = [(0, 0, 1), (0, 1, 0), (0, 1, 1), (1, 0, 0), (1, 0, 1), (1, 1, 0), (1, 1, 1)]
MESH = pl.DeviceIdType.MESH
ANY_SPEC = pl.BlockSpec(memory_space=pl.ANY)


def _me_and_peers():
    x, y, c = lax.axis_index("x"), lax.axis_index("y"), lax.axis_index("c")
    peers = [((1 - x) if fx else x, (1 - y) if fy else y, (1 - c) if fc else c) for fx, fy, fc in _FLIPS]
    return 4 * x + 2 * y + c, peers, [4 * px + 2 * py + pc for px, py, pc in peers]


def all_gather_hbm(shard, name):
    def body(x_ref, out_ref, send_sems, recv_sems, local_sem):
        x, y, c = lax.axis_index("x"), lax.axis_index("y"), lax.axis_index("c")
        me, sibling = (x, y, c), (x, y, 1 - c)
        chips = [(1 - x, y), (x, 1 - y), (1 - x, 1 - y)]

        def block(px, py, pc):
            return out_ref.at[4 * px + 2 * py + pc]

        def copy(i, blk, to, src=None):
            return pltpu.make_async_remote_copy(src_ref=block(*blk) if src is None else src, dst_ref=block(*blk),
                                                send_sem=send_sems.at[i], recv_sem=recv_sems.at[i],
                                                device_id=to, device_id_type=MESH)

        mine = pltpu.make_async_copy(x_ref, block(*me), local_sem)
        mine.start()
        first = [copy(0, me, sibling, src=x_ref)] + [copy(1 + j, me, (*chip, c), src=x_ref) for j, chip in enumerate(chips)]
        for cp in first:
            cp.start()
        passed = [copy(4 + j, (*chip, c), sibling) for j, chip in enumerate(chips)]
        for j, chip in enumerate(chips):
            copy(1 + j, (*chip, c), me).wait_recv()
            passed[j].start()
        copy(0, sibling, me).wait_recv()
        for j, chip in enumerate(chips):
            copy(4 + j, (*chip, 1 - c), me).wait_recv()
        for cp in first + passed:
            cp.wait_send()
        mine.wait()

    return pl.pallas_call(
        body, name=name,
        out_shape=jax.ShapeDtypeStruct((N_DEV,) + shard.shape, shard.dtype),
        in_specs=[ANY_SPEC], out_specs=ANY_SPEC,
        scratch_shapes=[pltpu.SemaphoreType.DMA((N_DEV - 1,)), pltpu.SemaphoreType.DMA((N_DEV - 1,)), pltpu.SemaphoreType.DMA],
    )(shard)


def all_to_all_hbm(blocks, name):
    def body(x_ref, out_ref, send_sems, recv_sems, local_sem):
        me, peers, peer_ids = _me_and_peers()
        mine = pltpu.make_async_copy(x_ref.at[me], out_ref.at[me], local_sem)
        mine.start()
        sends = [pltpu.make_async_remote_copy(src_ref=x_ref.at[pid], dst_ref=out_ref.at[me], send_sem=send_sems.at[i],
                                              recv_sem=recv_sems.at[i], device_id=p, device_id_type=MESH)
                 for i, (p, pid) in enumerate(zip(peers, peer_ids))]
        for cp in sends:
            cp.start()
        for i, (p, pid) in enumerate(zip(peers, peer_ids)):
            pltpu.make_async_remote_copy(src_ref=x_ref.at[me], dst_ref=out_ref.at[pid], send_sem=send_sems.at[i],
                                         recv_sem=recv_sems.at[i], device_id=p, device_id_type=MESH).wait_recv()
        for cp in sends:
            cp.wait_send()
        mine.wait()

    return pl.pallas_call(
        body, name=name,
        out_shape=jax.ShapeDtypeStruct(blocks.shape, blocks.dtype),
        in_specs=[ANY_SPEC], out_specs=ANY_SPEC,
        scratch_shapes=[pltpu.SemaphoreType.DMA((N_DEV - 1,)), pltpu.SemaphoreType.DMA((N_DEV - 1,)), pltpu.SemaphoreType.DMA],
    )(blocks)


def sum_blocks(blocks, name):
    _, r, c = blocks.shape
    ts = _row_tile(r, (N_DEV * blocks.dtype.itemsize + 4) * c)

    def body(x_ref, o_ref):
        acc = x_ref[0].astype(F32)
        for d in range(1, N_DEV):
            acc = acc + x_ref[d].astype(F32)
        o_ref[...] = acc

    return pl.pallas_call(
        body, name=name,
        out_shape=jax.ShapeDtypeStruct((r, c), F32),
        grid=(r // ts,),
        in_specs=[pl.BlockSpec((N_DEV, ts, c), lambda i: (0, i, 0))],
        out_specs=_row_spec(ts, c),
        compiler_params=_cparams(("parallel",), 2 * ts * c * (N_DEV * blocks.dtype.itemsize + 4)),
    )(blocks)


def all_reduce_small(v, name):
    r, c = v.shape

    def body(x_ref, out_ref, gathered, send_sems, recv_sems):
        me, peers, peer_ids = _me_and_peers()
        sends = [pltpu.make_async_remote_copy(src_ref=x_ref, dst_ref=gathered.at[me], send_sem=send_sems.at[i],
                                              recv_sem=recv_sems.at[i], device_id=p, device_id_type=MESH)
                 for i, p in enumerate(peers)]
        for cp in sends:
            cp.start()
        gathered[me] = x_ref[...]
        for i, (p, pid) in enumerate(zip(peers, peer_ids)):
            pltpu.make_async_remote_copy(src_ref=x_ref, dst_ref=gathered.at[pid], send_sem=send_sems.at[i],
                                         recv_sem=recv_sems.at[i], device_id=p, device_id_type=MESH).wait_recv()
        for cp in sends:
            cp.wait_send()
        acc = gathered[0]
        for d in range(1, N_DEV):
            acc = acc + gathered[d]
        out_ref[...] = acc

    return pl.pallas_call(
        body, name=name,
        out_shape=jax.ShapeDtypeStruct((r, c), F32),
        in_specs=[pl.BlockSpec(memory_space=pltpu.VMEM)],
        out_specs=pl.BlockSpec(memory_space=pltpu.VMEM),
        scratch_shapes=[pltpu.VMEM((N_DEV, r, c), F32), pltpu.SemaphoreType.DMA((N_DEV - 1,)), pltpu.SemaphoreType.DMA((N_DEV - 1,))],
    )(v)


_SHARD_AXIS = {"w_in": 1, "mla_w_qb": 1, "mla_w_kvb": 2, "w_mem_kv": 1, "w_merge_up": 2, "w_branch": 3, "w_out": 1}
_SMALL = ("mla_q_norm", "mla_kv_norm", "fox_forget_bias", "ln_gain", "ln_bias")

def _regroup_qb(w):
    lead = w.shape[:-1]
    t = w.reshape(lead + (N_HEADS, MLA_NOPE + MLA_ROPE))
    half = MLA_ROPE // 2
    parts = (t[..., :MLA_NOPE], t[..., MLA_NOPE:MLA_NOPE + half], t[..., MLA_NOPE + half:])
    return jnp.concatenate([p.reshape(lead + (-1,)) for p in parts], axis=-1)


def _ungroup_qb(g):
    lead = g.shape[:-1]
    half = MLA_ROPE // 2
    n_nope, n_half = N_HEADS * MLA_NOPE, N_HEADS * half
    parts = (g[..., :n_nope].reshape(lead + (N_HEADS, MLA_NOPE)), g[..., n_nope:n_nope + n_half].reshape(lead + (N_HEADS, half)),
             g[..., n_nope + n_half:].reshape(lead + (N_HEADS, half)))
    return jnp.concatenate(parts, axis=-1).reshape(lead + (-1,))


def _to_wire(name, w):
    if name == "w_in":
        return jnp.pad(w, ((0, 0), (0, 0), (0, IN_WIDTH_PAD - IN_WIDTH)))
    if name == "mla_w_qb":
        return _regroup_qb(w)
    return w


def _from_wire(name, g):
    if name == "w_in":
        return g[:, :, :IN_WIDTH]
    if name == "mla_w_qb":
        return _ungroup_qb(g)
    return g


def _join(name, blocks):
    a = _SHARD_AXIS[name]
    t = jnp.moveaxis(blocks, 0, a)
    return t.reshape(t.shape[:a] + (t.shape[a] * t.shape[a + 1],) + t.shape[a + 2:])


def _cut(name, full):
    a = _SHARD_AXIS[name]
    t = full.reshape(full.shape[:a] + (N_DEV, full.shape[a] // N_DEV) + full.shape[a + 1:])
    return jnp.moveaxis(t, a, 0)


def _heads(t):
    s = t.shape[0]
    return t.reshape(s, N_HEADS, -1).transpose(1, 0, 2)


def _unheads(t):
    return t.transpose(1, 0, 2).reshape(t.shape[1], -1)


def _layer(x, mem, rope, w, p, tag):
    s = x.shape[0]
    cos, sin, cos4, sin4 = rope
    c_q, c_kv, k_rope, sb_qkv, fox_qkv, fox_f, mem_q, gate_z, merge_r = in_proj(x, w["w_in"], IN_SPLITS, tag + "in_proj")

    (cq_n,) = rowwise(_rms_fn, [c_q], [p["mla_q_norm"]], [MLA_Q_RANK], tag + "q_rms")
    q_all = matmul(cq_n, w["mla_w_qb"], tag + "q_up")
    n_nope, n_half = N_HEADS * MLA_NOPE, N_HEADS * MLA_ROPE // 2
    q_nope, q_x1, q_x2 = q_all[:, :n_nope], q_all[:, n_nope:n_nope + n_half], q_all[:, n_nope + n_half:]
    (ckv_n,) = rowwise(_rms_fn, [c_kv], [p["mla_kv_norm"]], [MLA_KV_RANK], tag + "kv_rms")
    kv = matmul(ckv_n, w["mla_w_kvb"], tag + "kv_up").reshape(s, N_HEADS, MLA_NOPE + HEAD_DIM)
    k_nope, v_mla = kv[..., :MLA_NOPE], kv[..., MLA_NOPE:]
    half = MLA_ROPE // 2
    q_o1, q_o2 = rowwise(_rope_fn, [q_x1, q_x2, cos4, sin4], [], [n_half, n_half], tag + "q_rope")
    k_o1, k_o2 = rowwise(_rope_fn, [k_rope[:, :half], k_rope[:, half:], cos, sin], [], [half, half], tag + "k_rope")
    zeros = jnp.zeros((s, N_HEADS, MLA_QK_PAD - MLA_NOPE - MLA_ROPE), F32)
    q_mla = jnp.concatenate([q_nope.reshape(s, N_HEADS, MLA_NOPE), q_o1.reshape(s, N_HEADS, half),
                             q_o2.reshape(s, N_HEADS, half), zeros], axis=-1).transpose(1, 0, 2)
    k_mla = jnp.concatenate([k_nope, jnp.broadcast_to(k_o1[:, None, :], (s, N_HEADS, half)),
                             jnp.broadcast_to(k_o2[:, None, :], (s, N_HEADS, half)), zeros], axis=-1).transpose(1, 0, 2)
    y_mla = softmax_attention(q_mla, k_mla, v_mla.transpose(1, 0, 2), None,
                              (MLA_NOPE + MLA_ROPE) ** -0.5, True, tag + "mla_attn")

    width = N_HEADS * HEAD_DIM
    sq, sk, sv = (_heads(sb_qkv[:, i * width:(i + 1) * width]) for i in range(3))
    y_sb = stick_breaking_attention(sq, sk, sv, HEAD_DIM ** -0.5, tag + "stick_attn")

    fq, fk, fv = (_heads(fox_qkv[:, i * width:(i + 1) * width]) for i in range(3))
    (log_f,) = rowwise(_forget_fn, [fox_f], [p["fox_forget_bias"]], [N_HEADS], tag + "forget_gate")
    log_f8 = jnp.pad(log_f.T, ((0, SUBLANES - N_HEADS), (0, 0)))
    c = cumsum_lanes(log_f8, tag + "forget_cumsum")[:N_HEADS]
    y_fox = softmax_attention(fq, fk, fv, c, HEAD_DIM ** -0.5, True, tag + "fox_attn")

    mkv = matmul(mem, w["w_mem_kv"], tag + "mem_kv")
    y_mem = softmax_attention(_heads(mem_q), _heads(mkv[:, :width]), _heads(mkv[:, width:]), None,
                              HEAD_DIM ** -0.5, False, tag + "mem_attn")

    ys = [_unheads(t) for t in (y_mla, y_sb, y_fox, y_mem)]
    yb = rowwise(_silu_gate_fn, ys + [gate_z], [], [BRANCH_WIDTH] * N_BRANCHES, tag + "silu_gate")
    proj = [matmul(yb[n], w["w_branch"][n], tag + "branch%d" % n) for n in range(N_BRANCHES)]
    merged = merge_gate(merge_r, w["w_merge_up"], proj, tag + "merge")
    out = matmul(merged, w["w_out"], tag + "out_proj")
    (xn,) = rowwise(_deepnorm_fn, [x, out], [p["ln_gain"], p["ln_bias"]], [x.shape[1]], tag + "deepnorm")
    return xn


def _adamw(g, w, m, v, name):
    shape = g.shape
    two_d = (-1, shape[-1])
    g2, w2, m2, v2 = (t.reshape(two_d) for t in (g, w, m, v))
    delta, new_m, new_v = _rowwise_fwd_call(_adamw_fn, [g2, w2, m2, v2], [], [shape[-1]] * 3, name)
    return delta.reshape(shape), new_m.reshape(shape), new_v.reshape(shape)


def kernel(x, mem, positions, w_in, mla_q_norm, mla_w_qb, mla_kv_norm, mla_w_kvb, fox_forget_bias, w_mem_kv, w_merge_up, w_branch, w_out, ln_gain, ln_bias, loss_target, m_w_in, m_mla_q_norm, m_mla_w_qb, m_mla_kv_norm, m_mla_w_kvb, m_fox_forget_bias, m_w_mem_kv, m_w_merge_up, m_w_branch, m_w_out, m_ln_gain, m_ln_bias, v_w_in, v_mla_q_norm, v_mla_w_qb, v_mla_kv_norm, v_mla_w_kvb, v_fox_forget_bias, v_w_mem_kv, v_w_merge_up, v_w_branch, v_w_out, v_ln_gain, v_ln_bias):
    weights = dict(w_in=w_in, mla_q_norm=mla_q_norm, mla_w_qb=mla_w_qb, mla_kv_norm=mla_kv_norm, mla_w_kvb=mla_w_kvb,
                   fox_forget_bias=fox_forget_bias, w_mem_kv=w_mem_kv, w_merge_up=w_merge_up, w_branch=w_branch,
                   w_out=w_out, ln_gain=ln_gain, ln_bias=ln_bias)
    m_in = dict(w_in=m_w_in, mla_q_norm=m_mla_q_norm, mla_w_qb=m_mla_w_qb, mla_kv_norm=m_mla_kv_norm, mla_w_kvb=m_mla_w_kvb,
                fox_forget_bias=m_fox_forget_bias, w_mem_kv=m_w_mem_kv, w_merge_up=m_w_merge_up, w_branch=m_w_branch,
                w_out=m_w_out, ln_gain=m_ln_gain, ln_bias=m_ln_bias)
    v_in = dict(w_in=v_w_in, mla_q_norm=v_mla_q_norm, mla_w_qb=v_mla_w_qb, mla_kv_norm=v_mla_kv_norm, mla_w_kvb=v_mla_w_kvb,
                fox_forget_bias=v_fox_forget_bias, w_mem_kv=v_w_mem_kv, w_merge_up=v_w_merge_up, w_branch=v_w_branch,
                w_out=v_w_out, ln_gain=v_ln_gain, ln_bias=v_ln_bias)
    order = list(weights)
    big = list(_SHARD_AXIS)
    n_layers = w_in.shape[0]
    x0, mem0 = x[0], mem[0]
    s = x0.shape[0]

    wire = [_to_wire(n, weights[n]).astype(WIRE_DTYPE) for n in big]
    sizes = [t.size for t in wire]
    flat = jnp.concatenate([t.reshape(-1) for t in wire]).reshape(-1, LANES)
    gathered = all_gather_hbm(flat, "gather_weights").reshape(N_DEV, -1)
    offs = np.cumsum([0] + sizes)
    full = {n: _join(n, gathered[:, offs[i]:offs[i + 1]].reshape((N_DEV,) + wire[i].shape)) for i, n in enumerate(big)}
    per_layer_w = [{n: full[n][l] for n in big} for l in range(n_layers)]
    per_layer_p = [{n: weights[n][l][None, :] for n in _SMALL} for l in range(n_layers)]

    inv_freq = ROPE_THETA ** (-jnp.arange(0, MLA_ROPE, 2, dtype=F32) / MLA_ROPE)
    ang = positions[0].astype(F32)[:, None] * inv_freq
    cos, sin = jnp.cos(ang), jnp.sin(ang)
    rope = (cos, sin, jnp.tile(cos, (1, N_HEADS)), jnp.tile(sin, (1, N_HEADS)))

    def trunk(x, ws, ps):
        for l in range(n_layers):
            x = _layer(x, mem0, rope, ws[l], ps[l], "l%d_" % l)
        return x

    y, pullback = jax.vjp(trunk, x0, per_layer_w, per_layer_p)
    loss_part, dy = _loss_call(y, loss_target[0])
    grad_x, grad_ws, grad_ps = pullback(dy)

    cut = [_cut(n, jnp.stack([grad_ws[l][n] for l in range(n_layers)])) for n in big]
    blocks = jnp.concatenate([t.reshape(N_DEV, -1) for t in cut], axis=1).reshape(N_DEV, -1, LANES)
    received = all_to_all_hbm(blocks, "scatter_grads")
    summed = sum_blocks(received, "sum_grads").reshape(-1)
    grads = {n: _from_wire(n, summed[offs[i]:offs[i + 1]].reshape(wire[i].shape)) for i, n in enumerate(big)}

    small = [jnp.stack([grad_ps[l][n][0] for l in range(n_layers)]) for n in _SMALL]
    small_sizes = [t.size for t in small]
    packed = jnp.concatenate([t.reshape(-1) for t in small] + [loss_part.reshape(1)])
    rows = -(-packed.size // (SUBLANES * LANES)) * SUBLANES
    packed = jnp.pad(packed, (0, rows * LANES - packed.size)).reshape(rows, LANES)
    reduced = all_reduce_small(packed, "reduce_small").reshape(-1)
    small_offs = np.cumsum([0] + small_sizes)
    for i, n in enumerate(_SMALL):
        grads[n] = reduced[small_offs[i]:small_offs[i + 1]].reshape(small[i].shape)
    loss = reduced[small_offs[-1]]

    delta, new_m, new_v = {}, {}, {}
    for n in order:
        delta[n], new_m[n], new_v[n] = _adamw(grads[n], weights[n], m_in[n], v_in[n], "adamw_" + n)
    return (loss, grad_x[None], *[grads[n] for n in order], *[delta[n] for n in order],
            *[new_m[n] for n in order], *[new_v[n] for n in order])
```

```python
import functools

import numpy as np
import jax
import jax.numpy as jnp
from jax import lax
from jax.experimental import pallas as pl
from jax.experimental.pallas import tpu as pltpu

F32 = jnp.float32
MXU_DTYPE = jnp.bfloat16
WIRE_DTYPE = jnp.bfloat16

N_LAYERS = 4
N_BRANCHES = 4
BRANCH_WIDTH = 256
N_HEADS = 4
HEAD_DIM = 64
MLA_NOPE = 64
MLA_ROPE = 32
MLA_Q_RANK = 384
MLA_KV_RANK = 256
ROPE_THETA = 10000.0
MERGE_RANK = 128
RMS_EPS = 1e-6
LN_EPS = 1e-5
DEEPNORM_ALPHA = (2 * N_LAYERS) ** 0.25
IN_SPLITS = (MLA_Q_RANK, MLA_KV_RANK, MLA_ROPE, 3 * N_HEADS * HEAD_DIM, 3 * N_HEADS * HEAD_DIM,
             N_HEADS, N_HEADS * HEAD_DIM, N_BRANCHES * BRANCH_WIDTH, MERGE_RANK)
IN_WIDTH = sum(IN_SPLITS)
ADAM_LR, ADAM_B1, ADAM_B2, ADAM_EPS, ADAM_WD, ADAM_STEP = 0.001, 0.9, 0.999, 1e-08, 0.01, 10

LANES = 128
SUBLANES = 8
VMEM_BYTES_V7X = 64 * 1024 * 1024
VMEM_LIMIT_CAP = VMEM_BYTES_V7X - 8 * 1024 * 1024
MM_VMEM_BUDGET = 44 * 1024 * 1024
ROW_TILE_BYTES = 8 * 1024 * 1024
ATTN_TILE = 512
SUFFIX_BLOCK = 256
N_DEV = 8
IN_WIDTH_PAD = -(-IN_WIDTH // LANES) * LANES
MLA_QK_PAD = 128


def _cparams(semantics, vmem_estimate):
    limit = int(min(max(2 * vmem_estimate, 32 * 1024 * 1024), VMEM_LIMIT_CAP))
    return pltpu.CompilerParams(dimension_semantics=semantics, vmem_limit_bytes=limit)


def _dot(a, b, ca=1, cb=0):
    return lax.dot_general(a, b, (((ca,), (cb,)), ((), ())), preferred_element_type=F32)


def _div_tile(n, cap, align):
    t = (cap // align) * align
    while t >= align:
        if n % t == 0:
            return t
        t -= align
    return n


def _mm_call(a, b, ta, tb, out_dtype, name):
    k, m = a.shape if ta else a.shape[::-1]
    n, kb = b.shape if tb else b.shape[::-1]
    assert k == kb, (a.shape, b.shape, ta, tb)
    ab, bb, ob = a.dtype.itemsize, b.dtype.itemsize, jnp.dtype(out_dtype).itemsize
    tk = k if k <= 4096 else 512
    tn = n if n <= 4096 else _div_tile(n, 2048, LANES)
    cands = [t for t in (1024, 512, 256, 128) if t <= m and m % t == 0] or [m]
    for tm in cands:
        est = 2 * (tm * tk * ab + tk * tn * bb + tm * tn * ob) + (tm * tn * 4 if k > tk else 0)
        if est <= MM_VMEM_BUDGET:
            break
    nk = k // tk

    def body(a_ref, b_ref, o_ref, *scratch):
        p = _dot(a_ref[...].astype(MXU_DTYPE), b_ref[...].astype(MXU_DTYPE), 0 if ta else 1, 1 if tb else 0)
        if nk == 1:
            o_ref[...] = p.astype(o_ref.dtype)
        else:
            acc = scratch[0]
            kk = pl.program_id(2)

            @pl.when(kk == 0)
            def _():
                acc[...] = p

            @pl.when(kk > 0)
            def _():
                acc[...] += p

            @pl.when(kk == nk - 1)
            def _():
                o_ref[...] = acc[...].astype(o_ref.dtype)

    a_spec = pl.BlockSpec((tk, tm), lambda i, j, l: (l, i)) if ta else pl.BlockSpec((tm, tk), lambda i, j, l: (i, l))
    b_spec = pl.BlockSpec((tn, tk), lambda i, j, l: (j, l)) if tb else pl.BlockSpec((tk, tn), lambda i, j, l: (l, j))
    return pl.pallas_call(
        body, name=name,
        out_shape=jax.ShapeDtypeStruct((m, n), out_dtype),
        grid=(m // tm, n // tn, nk),
        in_specs=[a_spec, b_spec],
        out_specs=pl.BlockSpec((tm, tn), lambda i, j, l: (i, j)),
        scratch_shapes=[pltpu.VMEM((tm, tn), F32)] if nk > 1 else [],
        compiler_params=_cparams(("parallel", "parallel", "arbitrary"), est),
    )(a, b)


def matmul(a, b, name):
    @jax.custom_vjp
    def op(a, b):
        return _mm_call(a, b, False, False, F32, name)

    def fwd(a, b):
        return op(a, b), (a, b)

    def bwd(res, g):
        a, b = res
        da = _mm_call(g, b, False, True, a.dtype, name + "_da")
        db = _mm_call(a, g, True, False, b.dtype, name + "_db")
        return da, db

    op.defvjp(fwd, bwd)
    return op(a, b)


def _row_tile(n_rows, bytes_per_row):
    align = 2 * SUBLANES
    cap = max(align, ROW_TILE_BYTES // bytes_per_row)
    best = 0
    for t in range(align, min(n_rows, cap, 2048) + 1, align):
        if n_rows % t == 0:
            best = t
    return best or n_rows


def _row_spec(ts, cols):
    return pl.BlockSpec((ts, cols), lambda i: (i, 0))


def _full_spec(shape):
    return pl.BlockSpec(shape, lambda i: (0,) * len(shape))


def _rowwise_fwd_call(f, rows, params, out_cols, name):
    n = rows[0].shape[0]
    per_row = 4 * (sum(r.shape[1] for r in rows) + sum(out_cols))
    ts = _row_tile(n, per_row)
    n_in, n_p = len(rows), len(params)

    def body(*refs):
        ins = [r[...] for r in refs[:n_in + n_p]]
        outs = f(*ins)
        for o_ref, o in zip(refs[n_in + n_p:], outs):
            o_ref[...] = o.astype(o_ref.dtype)

    return pl.pallas_call(
        body, name=name,
        out_shape=[jax.ShapeDtypeStruct((n, c), F32) for c in out_cols],
        grid=(n // ts,),
        in_specs=[_row_spec(ts, r.shape[1]) for r in rows] + [_full_spec(p.shape) for p in params],
        out_specs=[_row_spec(ts, c) for c in out_cols],
        compiler_params=_cparams(("parallel",), 2 * ts * per_row),
    )(*rows, *params)


def _rowwise_bwd_call(f, rows, params, cts, name):
    n = rows[0].shape[0]
    per_row = 4 * (2 * sum(r.shape[1] for r in rows) + 2 * sum(c.shape[1] for c in cts))
    ts = _row_tile(n, per_row)
    n_in, n_p, n_ct = len(rows), len(params), len(cts)

    def body(*refs):
        ins = [r[...] for r in refs[:n_in + n_p]]
        ct = tuple(r[...] for r in refs[n_in + n_p:n_in + n_p + n_ct])
        grads = jax.vjp(lambda *a: tuple(f(*a)), *ins)[1](ct)
        out_refs = refs[n_in + n_p + n_ct:]
        for o_ref, g in zip(out_refs[:n_in], grads[:n_in]):
            o_ref[...] = g
        first = pl.program_id(0) == 0
        for o_ref, g in zip(out_refs[n_in:], grads[n_in:]):
            @pl.when(first)
            def _(o_ref=o_ref, g=g):
                o_ref[...] = g

            @pl.when(jnp.logical_not(first))
            def _(o_ref=o_ref, g=g):
                o_ref[...] += g

    outs = pl.pallas_call(
        body, name=name,
        out_shape=[jax.ShapeDtypeStruct(r.shape, F32) for r in rows] + [jax.ShapeDtypeStruct(p.shape, F32) for p in params],
        grid=(n // ts,),
        in_specs=([_row_spec(ts, r.shape[1]) for r in rows] + [_full_spec(p.shape) for p in params]
                  + [_row_spec(ts, c.shape[1]) for c in cts]),
        out_specs=[_row_spec(ts, r.shape[1]) for r in rows] + [_full_spec(p.shape) for p in params],
        compiler_params=_cparams(("arbitrary",), 2 * ts * per_row),
    )(*rows, *params, *cts)
    return outs[:n_in], outs[n_in:]


def rowwise(f, rows, params, out_cols, name):
    @jax.custom_vjp
    def op(rows, params):
        return tuple(_rowwise_fwd_call(f, rows, params, out_cols, name))

    def fwd(rows, params):
        return op(rows, params), (rows, params)

    def bwd(res, cts):
        rows, params = res
        d_rows, d_params = _rowwise_bwd_call(f, rows, params, cts, name + "_bwd")
        return tuple(d_rows), tuple(d_params)

    op.defvjp(fwd, bwd)
    return op(tuple(rows), tuple(params))


def _rms_fn(x, g):
    return (x * lax.rsqrt(jnp.mean(x * x, axis=-1, keepdims=True) + RMS_EPS) * g,)


def _rope_fn(x1, x2, cos, sin):
    return x1 * cos - x2 * sin, x1 * sin + x2 * cos


def _log_sigmoid(x):
    return jnp.minimum(x, 0.0) - jnp.log1p(jnp.exp(-jnp.abs(x)))


def _forget_fn(f, bias):
    return (_log_sigmoid(f + bias),)


def _silu_gate_fn(y0, y1, y2, y3, gate_z):
    return tuple(y * (z * jax.nn.sigmoid(z))
                 for y, z in zip((y0, y1, y2, y3), (gate_z[:, n * BRANCH_WIDTH:(n + 1) * BRANCH_WIDTH] for n in range(N_BRANCHES))))


def _deepnorm_fn(x, out, g, b):
    h = DEEPNORM_ALPHA * x + out
    mu = jnp.mean(h, axis=-1, keepdims=True)
    var = jnp.mean(jnp.square(h - mu), axis=-1, keepdims=True)
    return ((h - mu) * lax.rsqrt(var + LN_EPS) * g + b,)


def _adamw_fn(g, w, m, v):
    m = ADAM_B1 * m + (1.0 - ADAM_B1) * g
    v = ADAM_B2 * v + (1.0 - ADAM_B2) * jnp.square(g)
    m_hat = m / (1.0 - ADAM_B1 ** ADAM_STEP)
    v_hat = v / (1.0 - ADAM_B2 ** ADAM_STEP)
    delta = -ADAM_LR * (m_hat / (jnp.sqrt(v_hat) + ADAM_EPS) + ADAM_WD * w)
    return delta, m, v


def in_proj(x, w, sizes, name):
    offs = np.cumsum((0,) + tuple(sizes))
    pad = w.shape[1] - int(offs[-1])

    @jax.custom_vjp
    def op(x, w):
        h = _mm_call(x, w, False, False, F32, name)
        return tuple(h[:, offs[i]:offs[i + 1]] for i in range(len(sizes)))

    def fwd(x, w):
        return op(x, w), (x, w)

    def bwd(res, cts):
        x, w = res
        parts = [c.astype(MXU_DTYPE) for c in cts] + ([jnp.zeros((x.shape[0], pad), MXU_DTYPE)] if pad else [])
        dh = jnp.concatenate(parts, axis=1)
        return _mm_call(dh, w, False, True, x.dtype, name + "_da"), _mm_call(x, dh, True, False, w.dtype, name + "_db")

    op.defvjp(fwd, bwd)
    return op(x, w)


MERGE_ROWS = 256


def _merge_fwd_call(r, w_up, ps, name):
    n, rank = r.shape
    d = ps[0].shape[1]
    nb = len(ps)
    ts = min(MERGE_ROWS, n)

    def body(r_ref, w_ref, *refs):
        rb = r_ref[...].astype(MXU_DTYPE)
        acc = jnp.zeros((ts, d), F32)
        for i in range(nb):
            gate = jax.nn.sigmoid(_dot(rb, w_ref[:, i * d:(i + 1) * d]))
            acc = acc + gate * refs[i][...]
        refs[nb][...] = acc

    return pl.pallas_call(
        body, name=name,
        out_shape=jax.ShapeDtypeStruct((n, d), F32),
        grid=(n // ts,),
        in_specs=[_row_spec(ts, rank), _full_spec(w_up.shape)] + [_row_spec(ts, d)] * nb,
        out_specs=_row_spec(ts, d),
        compiler_params=_cparams(("parallel",), 2 * 4 * ts * d * (nb + 2)),
    )(r, w_up, *ps)


def _merge_bwd_call(r, w_up, ps, dm, name):
    n, rank = r.shape
    d = ps[0].shape[1]
    nb = len(ps)
    ts = min(MERGE_ROWS, n)
    steps = n // ts

    def body(r_ref, w_ref, *refs):
        p_refs, dm_ref = refs[:nb], refs[nb]
        dr_ref, dw_ref, dp_refs, dw_acc = refs[nb + 1], refs[nb + 2], refs[nb + 3:2 * nb + 3], refs[2 * nb + 3]
        step = pl.program_id(0)

        @pl.when(step == 0)
        def _():
            dw_acc[...] = jnp.zeros_like(dw_acc)

        rb = r_ref[...].astype(MXU_DTYPE)
        dmv = dm_ref[...]
        dr = jnp.zeros((ts, rank), F32)
        for i in range(nb):
            w_i = w_ref[:, i * d:(i + 1) * d]
            gate = jax.nn.sigmoid(_dot(rb, w_i))
            dp_refs[i][...] = dmv * gate
            d_logit = (dmv * p_refs[i][...] * gate * (1.0 - gate)).astype(MXU_DTYPE)
            dr = dr + _dot(d_logit, w_i, 1, 1)
            dw_acc[:, i * d:(i + 1) * d] += _dot(rb, d_logit, 0, 0)
        dr_ref[...] = dr

        @pl.when(step == steps - 1)
        def _():
            dw_ref[...] = dw_acc[...].astype(dw_ref.dtype)

    outs = pl.pallas_call(
        body, name=name,
        out_shape=[jax.ShapeDtypeStruct((n, rank), F32), jax.ShapeDtypeStruct(w_up.shape, w_up.dtype)]
        + [jax.ShapeDtypeStruct((n, d), F32)] * nb,
        grid=(steps,),
        in_specs=[_row_spec(ts, rank), _full_spec(w_up.shape)] + [_row_spec(ts, d)] * (nb + 1),
        out_specs=[_row_spec(ts, rank), _full_spec(w_up.shape)] + [_row_spec(ts, d)] * nb,
        scratch_shapes=[pltpu.VMEM(w_up.shape, F32)],
        compiler_params=_cparams(("arbitrary",), 2 * 4 * ts * d * (2 * nb + 2) + 8 * w_up.size),
    )(r, w_up, *ps, dm)
    return outs[0], outs[1], outs[2:]


def merge_gate(r, w_up, ps, name):
    @jax.custom_vjp
    def op(r, w_up, ps):
        return _merge_fwd_call(r, w_up, ps, name)

    def fwd(r, w_up, ps):
        return op(r, w_up, ps), (r, w_up, ps)

    def bwd(res, dm):
        r, w_up, ps = res
        dr, dw, dps = _merge_bwd_call(r, w_up, ps, dm, name + "_bwd")
        return dr, dw, tuple(dps)

    op.defvjp(fwd, bwd)
    return op(r, w_up, tuple(ps))


LOG2E = 1.4426950408889634


def _to_lanes(t):
    return t.transpose(0, 2, 1)


def _key_tiles(t, tk):
    h, s, d = t.shape
    return t.reshape(h, s // tk, tk, d).transpose(0, 1, 3, 2)


def _attn_vmem(sk, dq, dv, tq, tk, backward):
    resident = 2 * 2 * sk * (dq + dv) * (2 if backward else 1)
    grads = 2 * 4 * sk * (dq + dv) if backward else 0
    return resident + grads + 24 * tq * tk * 4


def _tile_ids(q0, k0, tq, tk):
    key = k0 + lax.broadcasted_iota(jnp.int32, (tk, tq), 0)
    qry = q0 + lax.broadcasted_iota(jnp.int32, (tk, tq), 1)
    return key, qry


def _lane_spec(d, tq):
    return pl.BlockSpec((1, d, tq), lambda hh, i: (hh, 0, i))


def _row_major_spec(tq, d):
    return pl.BlockSpec((1, tq, d), lambda hh, i: (hh, i, 0))


def _resident_spec(shape):
    return pl.BlockSpec((1,) + shape, lambda hh, i: (hh,) + (0,) * len(shape))


def _softmax_fwd_call(qt, k, vt, causal, name):
    h, dq, s = qt.shape
    sk, (n_kb, dv, tk) = k.shape[1], vt.shape[1:]
    tq = min(ATTN_TILE, s)
    assert not causal or (tq == tk and s == sk)

    def body(q_ref, k_ref, v_ref, o_ref, lse_ref):
        qi = pl.program_id(1)
        q_t = q_ref[0]

        def tile(kb, carry, masked):
            m, l, acc = carry
            off = pl.multiple_of(kb * tk, tk)
            st = _dot(k_ref[0, pl.ds(off, tk), :], q_t)
            if masked:
                key, qry = _tile_ids(qi * tq, kb * tk, tq, tk)
                st = jnp.where(key <= qry, st, -jnp.inf)
            m_new = jnp.maximum(m, jnp.max(st, axis=0, keepdims=True))
            alpha = jnp.exp2(m - m_new)
            p = jnp.exp2(st - m_new)
            l = alpha * l + jnp.sum(p, axis=0, keepdims=True)
            acc = alpha * acc + _dot(v_ref[0, kb], p.astype(MXU_DTYPE))
            return m_new, l, acc

        carry = (jnp.full((1, tq), -jnp.inf, F32), jnp.zeros((1, tq), F32), jnp.zeros((dv, tq), F32))
        carry = lax.fori_loop(0, qi if causal else n_kb, lambda kb, c: tile(kb, c, False), carry)
        if causal:
            carry = tile(qi, carry, True)
        m, l, acc = carry
        o_ref[0] = acc / l
        lse_ref[0] = m + jnp.log2(l)

    return pl.pallas_call(
        body, name=name,
        out_shape=[jax.ShapeDtypeStruct((h, dv, s), F32), jax.ShapeDtypeStruct((h, 1, s), F32)],
        grid=(h, s // tq),
        in_specs=[_lane_spec(dq, tq), _resident_spec((sk, dq)), _resident_spec((n_kb, dv, tk))],
        out_specs=[_lane_spec(dv, tq), _lane_spec(1, tq)],
        compiler_params=_cparams(("parallel", "arbitrary"), _attn_vmem(sk, dq, dv, tq, tk, False)),
    )(qt, k, vt)


def _softmax_bwd_call(qt, q, k, kt, v, ot, lse, dot, do, causal, name):
    h, dq, s = qt.shape
    sk, dv = k.shape[1], v.shape[2]
    n_kb, tk = kt.shape[1], kt.shape[3]
    tq = min(ATTN_TILE, s)

    def body(qt_ref, q_ref, k_ref, kt_ref, v_ref, ot_ref, lse_ref, dot_ref, do_ref, gq_ref, gk_ref, gv_ref):
        qi = pl.program_id(1)

        @pl.when(qi == 0)
        def _():
            gk_ref[...] = jnp.zeros_like(gk_ref)
            gv_ref[...] = jnp.zeros_like(gv_ref)

        q_t, q_n, do_n = qt_ref[0], q_ref[0], do_ref[0]
        do_t = dot_ref[0]
        do_tb = do_t.astype(MXU_DTYPE)
        delta = jnp.sum(do_t * ot_ref[0], axis=0, keepdims=True)
        lse_q = lse_ref[0]

        def tile(kb, gq_acc, masked):
            off = pl.multiple_of(kb * tk, tk)
            st = _dot(k_ref[0, pl.ds(off, tk), :], q_t)
            if masked:
                key, qry = _tile_ids(qi * tq, kb * tk, tq, tk)
                st = jnp.where(key <= qry, st, -jnp.inf)
            p = jnp.exp2(st - lse_q)
            dp = _dot(v_ref[0, pl.ds(off, tk), :], do_tb)
            ds = (p * (dp - delta)).astype(MXU_DTYPE)
            gk_ref[0, pl.ds(off, tk), :] += _dot(ds, q_n)
            gv_ref[0, pl.ds(off, tk), :] += _dot(p.astype(MXU_DTYPE), do_n)
            return gq_acc + _dot(kt_ref[0, kb], ds)

        gq = lax.fori_loop(0, qi if causal else n_kb, lambda kb, c: tile(kb, c, False), jnp.zeros((dq, tq), F32))
        if causal:
            gq = tile(qi, gq, True)
        gq_ref[0] = gq

    return pl.pallas_call(
        body, name=name,
        out_shape=[jax.ShapeDtypeStruct((h, dq, s), F32), jax.ShapeDtypeStruct((h, sk, dq), F32), jax.ShapeDtypeStruct((h, sk, dv), F32)],
        grid=(h, s // tq),
        in_specs=[_lane_spec(dq, tq), _row_major_spec(tq, dq), _resident_spec((sk, dq)), _resident_spec((n_kb, dq, tk)),
                  _resident_spec((sk, dv)), _lane_spec(dv, tq), _lane_spec(1, tq), _lane_spec(dv, tq), _row_major_spec(tq, dv)],
        out_specs=[_lane_spec(dq, tq), _resident_spec((sk, dq)), _resident_spec((sk, dv))],
        compiler_params=_cparams(("parallel", "arbitrary"), _attn_vmem(sk, dq, dv, tq, tk, True)),
    )(qt, q, k, kt, v, ot, lse, dot, do)


def _split_terms(x, n):
    info = jnp.finfo(MXU_DTYPE)
    terms = []
    for _ in range(n):
        t = lax.reduce_precision(x, info.nexp, info.nmant)
        terms.append(t)
        x = x - t
    return terms


def softmax_attention(q, k, v, c, scale, causal, name):
    decay = c is not None
    d = q.shape[2]
    tk = min(ATTN_TILE, k.shape[1])

    def widen(q, k, c):
        qs = q * (scale * LOG2E)
        if decay:
            terms = jnp.stack(_split_terms(c * LOG2E, 3), axis=-1)
            ones = jnp.ones_like(terms)
            pad = jnp.zeros(q.shape[:2] + (LANES - d - 6,), F32)
            qs = jnp.concatenate([qs, terms, ones, pad], axis=-1)
            k = jnp.concatenate([k, ones, -terms, pad], axis=-1)
        return qs.astype(MXU_DTYPE), k.astype(MXU_DTYPE)

    def run(q, k, v, c):
        qa, ka = widen(q, k, c)
        vb = v.astype(MXU_DTYPE)
        qt = _to_lanes(qa)
        ot, lse = _softmax_fwd_call(qt, ka, _key_tiles(vb, tk), causal, name)
        return _to_lanes(ot), (qt, qa, ka, vb, ot, lse)

    @jax.custom_vjp
    def op(q, k, v, c):
        return run(q, k, v, c)[0]

    def bwd(res, do):
        qt, qa, ka, vb, ot, lse = res
        gqt, gk, gv = _softmax_bwd_call(qt, qa, ka, _key_tiles(ka, tk), vb, ot, lse, _to_lanes(do),
                                        do.astype(MXU_DTYPE), causal, name + "_bwd")
        gq = _to_lanes(gqt)
        dc = gq[..., d] - gk[..., d + 3] if decay else None
        return gq[..., :d] * scale, gk[..., :d] * (1.0 / LOG2E), gv, dc

    op.defvjp(run, bwd)
    return op(q, k, v, c)


def _running_sum(ones, x, suffix):
    xb = x.astype(MXU_DTYPE)
    b = ones.shape[0]
    n = x.shape[0] // b
    out, carry = [None] * n, None
    for i in (reversed(range(n)) if suffix else range(n)):
        blk = _dot(ones, xb[i * b:(i + 1) * b])
        blk = blk if carry is None else blk + carry
        carry = blk[0:1, :] if suffix else blk[b - 1:b, :]
        out[i] = blk
    return out[0] if n == 1 else jnp.concatenate(out, axis=0)


def _ones_after(tk):
    b = min(tk, SUFFIX_BLOCK)
    return (lax.broadcasted_iota(jnp.int32, (b, b), 1) >= lax.broadcasted_iota(jnp.int32, (b, b), 0)).astype(MXU_DTYPE)


def _ones_before(tk):
    b = min(tk, SUFFIX_BLOCK)
    return (lax.broadcasted_iota(jnp.int32, (b, b), 1) <= lax.broadcasted_iota(jnp.int32, (b, b), 0)).astype(MXU_DTYPE)


LOGIT_CLAMP = 120.0


def _log_keep(k_tile, q_t, strict):
    z = jnp.minimum(_dot(k_tile, q_t), LOGIT_CLAMP)
    log_keep = -(jnp.log(1.0 + jnp.exp2(z)) * LOG2E)
    return z, (log_keep if strict is None else jnp.where(strict, log_keep, 0.0))


def _stick_weights(z, log_keep, later, ones_after, strict):
    local = _running_sum(ones_after, log_keep, True)
    a = jnp.exp2(z + local + later)
    return (a if strict is None else jnp.where(strict, a, 0.0)), local


def _strict_mask(qi, kb, t):
    key, qry = _tile_ids(qi * t, kb * t, t, t)
    return key < qry


def _stick_fwd_call(qt, k, vt, name):
    h, d, s = qt.shape
    t = vt.shape[3]
    n_t = s // t
    assert t == min(ATTN_TILE, s)

    def body(q_ref, k_ref, v_ref, o_ref, later_ref):
        qi = pl.program_id(1)
        q_t = q_ref[0]
        ones_after = _ones_after(t)
        later_ref[...] = jnp.zeros_like(later_ref)

        def tile(kb, carry, strict):
            later, acc = carry
            off = pl.multiple_of(kb * t, t)
            z, log_keep = _log_keep(k_ref[0, pl.ds(off, t), :], q_t, strict)
            a, local = _stick_weights(z, log_keep, later, ones_after, strict)
            later_ref[0, kb] = later
            return later + local[0:1, :], acc + _dot(v_ref[0, kb], a.astype(MXU_DTYPE))

        carry = tile(qi, (jnp.zeros((1, t), F32), jnp.zeros((d, t), F32)), _strict_mask(qi, qi, t))
        _, acc = lax.fori_loop(0, qi, lambda i, c: tile(qi - 1 - i, c, None), carry)
        o_ref[0] = acc

    later_spec = pl.BlockSpec((1, n_t, 1, t), lambda hh, i: (hh, 0, 0, i))
    return pl.pallas_call(
        body, name=name,
        out_shape=[jax.ShapeDtypeStruct((h, d, s), F32), jax.ShapeDtypeStruct((h, n_t, 1, s), F32)],
        grid=(h, n_t),
        in_specs=[_lane_spec(d, t), _resident_spec((s, d)), _resident_spec((n_t, d, t))],
        out_specs=[_lane_spec(d, t), later_spec],
        compiler_params=_cparams(("parallel", "arbitrary"), _attn_vmem(s, d, d, t, t, False)),
    )(qt, k, vt)


def _stick_bwd_call(qt, q, k, kt, v, later, dot, do, name):
    h, d, s = qt.shape
    t = kt.shape[3]
    n_t = s // t

    def body(qt_ref, q_ref, k_ref, kt_ref, v_ref, later_ref, dot_ref, do_ref, gq_ref, gk_ref, gv_ref):
        qi = pl.program_id(1)

        @pl.when(qi == 0)
        def _():
            gk_ref[...] = jnp.zeros_like(gk_ref)
            gv_ref[...] = jnp.zeros_like(gv_ref)

        q_t, q_n, do_n = qt_ref[0], q_ref[0], do_ref[0]
        do_tb = dot_ref[0].astype(MXU_DTYPE)
        ones_after, ones_before = _ones_after(t), _ones_before(t)
        diagonal = _strict_mask(qi, qi, t)

        def sweep_right(kb, carry, strict):
            before, gq_acc = carry
            off = pl.multiple_of(kb * t, t)
            z, log_keep = _log_keep(k_ref[0, pl.ds(off, t), :], q_t, strict)
            a, _ = _stick_weights(z, log_keep, later_ref[0, kb], ones_after, strict)
            g = a * _dot(v_ref[0, pl.ds(off, t), :], do_tb)
            g_prefix = _running_sum(ones_before, g, False) + before
            sig = 1.0 - jnp.exp2(log_keep)
            dz = g - sig * g_prefix
            dz = (dz if strict is None else jnp.where(strict, dz, 0.0)).astype(MXU_DTYPE)
            gk_ref[0, pl.ds(off, t), :] += _dot(dz, q_n)
            gv_ref[0, pl.ds(off, t), :] += _dot(a.astype(MXU_DTYPE), do_n)
            return g_prefix[t - 1:t, :], gq_acc + _dot(kt_ref[0, kb], dz)

        carry = lax.fori_loop(0, qi, lambda kb, c: sweep_right(kb, c, None), (jnp.zeros((1, t), F32), jnp.zeros((d, t), F32)))
        _, gq = sweep_right(qi, carry, diagonal)
        gq_ref[0] = gq

    return pl.pallas_call(
        body, name=name,
        out_shape=[jax.ShapeDtypeStruct((h, d, s), F32), jax.ShapeDtypeStruct((h, s, d), F32), jax.ShapeDtypeStruct((h, s, d), F32)],
        grid=(h, n_t),
        in_specs=[_lane_spec(d, t), _row_major_spec(t, d), _resident_spec((s, d)), _resident_spec((n_t, d, t)),
                  _resident_spec((s, d)), pl.BlockSpec((1, n_t, 1, t), lambda hh, i: (hh, 0, 0, i)),
                  _lane_spec(d, t), _row_major_spec(t, d)],
        out_specs=[_lane_spec(d, t), _resident_spec((s, d)), _resident_spec((s, d))],
        compiler_params=_cparams(("parallel", "arbitrary"), _attn_vmem(s, d, d, t, t, True)),
    )(qt, q, k, kt, v, later, dot, do)


def stick_breaking_attention(q, k, v, scale, name):
    t = min(ATTN_TILE, q.shape[1])

    def run(q, k, v):
        qb, kb, vb = (q * (scale * LOG2E)).astype(MXU_DTYPE), k.astype(MXU_DTYPE), v.astype(MXU_DTYPE)
        qt = _to_lanes(qb)
        ot, later = _stick_fwd_call(qt, kb, _key_tiles(vb, t), name)
        return _to_lanes(ot), (qt, qb, kb, vb, later)

    @jax.custom_vjp
    def op(q, k, v):
        return run(q, k, v)[0]

    def bwd(res, do):
        qt, qb, kb, vb, later = res
        gqt, gk, gv = _stick_bwd_call(qt, qb, kb, _key_tiles(kb, t), vb, later, _to_lanes(do), do.astype(MXU_DTYPE), name + "_bwd")
        return _to_lanes(gqt) * scale, gk * (1.0 / LOG2E), gv

    op.defvjp(run, bwd)
    return op(q, k, v)


def _cumsum_call(x, reverse, name):
    r, s = x.shape
    tb = min(ATTN_TILE, s)
    nb = s // tb

    def body(x_ref, o_ref, carry):
        @pl.when(pl.program_id(0) == 0)
        def _():
            carry[...] = jnp.zeros_like(carry)

        j = lax.broadcasted_iota(jnp.int32, (tb, tb), 0)
        t = lax.broadcasted_iota(jnp.int32, (tb, tb), 1)
        ones = ((j >= t) if reverse else (j <= t)).astype(MXU_DTYPE)
        rest = x_ref[...]
        y = carry[:, 0:1]
        for _ in range(3):
            term = rest.astype(MXU_DTYPE)
            rest = rest - term.astype(F32)
            y = y + _dot(term, ones)
        o_ref[...] = y
        carry[...] = jnp.broadcast_to(y[:, 0:1] if reverse else y[:, tb - 1:tb], carry.shape)

    spec = pl.BlockSpec((r, tb), (lambda i: (0, nb - 1 - i)) if reverse else (lambda i: (0, i)))
    return pl.pallas_call(
        body, name=name,
        out_shape=jax.ShapeDtypeStruct((r, s), F32),
        grid=(nb,),
        in_specs=[spec], out_specs=spec,
        scratch_shapes=[pltpu.VMEM((r, LANES), F32)],
        compiler_params=_cparams(("arbitrary",), 1024 * 1024),
    )(x)


def cumsum_lanes(x, name):
    @jax.custom_vjp
    def op(x):
        return _cumsum_call(x, False, name)

    op.defvjp(lambda x: (op(x), None), lambda _, g: (_cumsum_call(g, True, name + "_bwd"),))
    return op(x)


def _loss_call(y, target):
    n, d = y.shape
    ts = _row_tile(n, 4 * 3 * d)

    def body(y_ref, t_ref, dy_ref, loss_ref):
        err = y_ref[...] - t_ref[...]
        dy_ref[...] = err * (1.0 / d)
        part = jnp.sum(jnp.sum(err * err, axis=1, keepdims=True), axis=0, keepdims=True) * (0.5 / d)

        @pl.when(pl.program_id(0) == 0)
        def _():
            loss_ref[...] = jnp.zeros_like(loss_ref)

        loss_ref[...] += jnp.broadcast_to(part, loss_ref.shape)

    dy, loss = pl.pallas_call(
        body, name="loss_head",
        out_shape=[jax.ShapeDtypeStruct((n, d), F32), jax.ShapeDtypeStruct((SUBLANES, LANES), F32)],
        grid=(n // ts,),
        in_specs=[_row_spec(ts, d), _row_spec(ts, d)],
        out_specs=[_row_spec(ts, d), _full_spec((SUBLANES, LANES))],
        compiler_params=_cparams(("arbitrary",), 2 * ts * 4 * 3 * d),
    )(y, target)
    return loss[0, 0], dy


_FLIPS = [(0, 0, 1), (0, 1, 0), (0, 1, 1), (1, 0, 0), (1, 0, 1), (1, 1, 0), (1, 1, 1)]
MESH = pl.DeviceIdType.MESH
ANY_SPEC = pl.BlockSpec(memory_space=pl.ANY)


def _me_and_peers():
    x, y, c = lax.axis_index("x"), lax.axis_index("y"), lax.axis_index("c")
    peers = [((1 - x) if fx else x, (1 - y) if fy else y, (1 - c) if fc else c) for fx, fy, fc in _FLIPS]
    return 4 * x + 2 * y + c, peers, [4 * px + 2 * py + pc for px, py, pc in peers]


def all_gather_hbm(shard, name):
    def body(x_ref, out_ref, send_sems, recv_sems, local_sem):
        x, y, c = lax.axis_index("x"), lax.axis_index("y"), lax.axis_index("c")
        me, sibling = (x, y, c), (x, y, 1 - c)
        chips = [(1 - x, y), (x, 1 - y), (1 - x, 1 - y)]

        def block(px, py, pc):
            return out_ref.at[4 * px + 2 * py + pc]

        def copy(i, blk, to, src=None):
            return pltpu.make_async_remote_copy(src_ref=block(*blk) if src is None else src, dst_ref=block(*blk),
                                                send_sem=send_sems.at[i], recv_sem=recv_sems.at[i],
                                                device_id=to, device_id_type=MESH)

        mine = pltpu.make_async_copy(x_ref, block(*me), local_sem)
        mine.start()
        first = [copy(0, me, sibling, src=x_ref)] + [copy(1 + j, me, (*chip, c), src=x_ref) for j, chip in enumerate(chips)]
        for cp in first:
            cp.start()
        passed = [copy(4 + j, (*chip, c), sibling) for j, chip in enumerate(chips)]
        for j, chip in enumerate(chips):
            copy(1 + j, (*chip, c), me).wait_recv()
            passed[j].start()
        copy(0, sibling, me).wait_recv()
        for j, chip in enumerate(chips):
            copy(4 + j, (*chip, 1 - c), me).wait_recv()
        for cp in first + passed:
            cp.wait_send()
        mine.wait()

    return pl.pallas_call(
        body, name=name,
        out_shape=jax.ShapeDtypeStruct((N_DEV,) + shard.shape, shard.dtype),
        in_specs=[ANY_SPEC], out_specs=ANY_SPEC,
        scratch_shapes=[pltpu.SemaphoreType.DMA((N_DEV - 1,)), pltpu.SemaphoreType.DMA((N_DEV - 1,)), pltpu.SemaphoreType.DMA],
    )(shard)


def all_to_all_hbm(blocks, name):
    def body(x_ref, out_ref, send_sems, recv_sems, local_sem):
        me, peers, peer_ids = _me_and_peers()
        mine = pltpu.make_async_copy(x_ref.at[me], out_ref.at[me], local_sem)
        mine.start()
        sends = [pltpu.make_async_remote_copy(src_ref=x_ref.at[pid], dst_ref=out_ref.at[me], send_sem=send_sems.at[i],
                                              recv_sem=recv_sems.at[i], device_id=p, device_id_type=MESH)
                 for i, (p, pid) in enumerate(zip(peers, peer_ids))]
        for cp in sends:
            cp.start()
        for i, (p, pid) in enumerate(zip(peers, peer_ids)):
            pltpu.make_async_remote_copy(src_ref=x_ref.at[me], dst_ref=out_ref.at[pid], send_sem=send_sems.at[i],
                                         recv_sem=recv_sems.at[i], device_id=p, device_id_type=MESH).wait_recv()
        for cp in sends:
            cp.wait_send()
        mine.wait()

    return pl.pallas_call(
        body, name=name,
        out_shape=jax.ShapeDtypeStruct(blocks.shape, blocks.dtype),
        in_specs=[ANY_SPEC], out_specs=ANY_SPEC,
        scratch_shapes=[pltpu.SemaphoreType.DMA((N_DEV - 1,)), pltpu.SemaphoreType.DMA((N_DEV - 1,)), pltpu.SemaphoreType.DMA],
    )(blocks)


def sum_blocks(blocks, name):
    _, r, c = blocks.shape
    ts = _row_tile(r, (N_DEV * blocks.dtype.itemsize + 4) * c)

    def body(x_ref, o_ref):
        acc = x_ref[0].astype(F32)
        for d in range(1, N_DEV):
            acc = acc + x_ref[d].astype(F32)
        o_ref[...] = acc

    return pl.pallas_call(
        body, name=name,
        out_shape=jax.ShapeDtypeStruct((r, c), F32),
        grid=(r // ts,),
        in_specs=[pl.BlockSpec((N_DEV, ts, c), lambda i: (0, i, 0))],
        out_specs=_row_spec(ts, c),
        compiler_params=_cparams(("parallel",), 2 * ts * c * (N_DEV * blocks.dtype.itemsize + 4)),
    )(blocks)


def all_reduce_small(v, name):
    r, c = v.shape

    def body(x_ref, out_ref, gathered, send_sems, recv_sems):
        me, peers, peer_ids = _me_and_peers()
        sends = [pltpu.make_async_remote_copy(src_ref=x_ref, dst_ref=gathered.at[me], send_sem=send_sems.at[i],
                                              recv_sem=recv_sems.at[i], device_id=p, device_id_type=MESH)
                 for i, p in enumerate(peers)]
        for cp in sends:
            cp.start()
        gathered[me] = x_ref[...]
        for i, (p, pid) in enumerate(zip(peers, peer_ids)):
            pltpu.make_async_remote_copy(src_ref=x_ref, dst_ref=gathered.at[pid], send_sem=send_sems.at[i],
                                         recv_sem=recv_sems.at[i], device_id=p, device_id_type=MESH).wait_recv()
        for cp in sends:
            cp.wait_send()
        acc = gathered[0]
        for d in range(1, N_DEV):
            acc = acc + gathered[d]
        out_ref[...] = acc

    return pl.pallas_call(
        body, name=name,
        out_shape=jax.ShapeDtypeStruct((r, c), F32),
        in_specs=[pl.BlockSpec(memory_space=pltpu.VMEM)],
        out_specs=pl.BlockSpec(memory_space=pltpu.VMEM),
        scratch_shapes=[pltpu.VMEM((N_DEV, r, c), F32), pltpu.SemaphoreType.DMA((N_DEV - 1,)), pltpu.SemaphoreType.DMA((N_DEV - 1,))],
    )(v)


_SHARD_AXIS = {"w_in": 1, "mla_w_qb": 1, "mla_w_kvb": 2, "w_mem_kv": 1, "w_merge_up": 2, "w_branch": 3, "w_out": 1}
_SMALL = ("mla_q_norm", "mla_kv_norm", "fox_forget_bias", "ln_gain", "ln_bias")

def _regroup_qb(w):
    lead = w.shape[:-1]
    t = w.reshape(lead + (N_HEADS, MLA_NOPE + MLA_ROPE))
    half = MLA_ROPE // 2
    parts = (t[..., :MLA_NOPE], t[..., MLA_NOPE:MLA_NOPE + half], t[..., MLA_NOPE + half:])
    return jnp.concatenate([p.reshape(lead + (-1,)) for p in parts], axis=-1)


def _ungroup_qb(g):
    lead = g.shape[:-1]
    half = MLA_ROPE // 2
    n_nope, n_half = N_HEADS * MLA_NOPE, N_HEADS * half
    parts = (g[..., :n_nope].reshape(lead + (N_HEADS, MLA_NOPE)), g[..., n_nope:n_nope + n_half].reshape(lead + (N_HEADS, half)),
             g[..., n_nope + n_half:].reshape(lead + (N_HEADS, half)))
    return jnp.concatenate(parts, axis=-1).reshape(lead + (-1,))


def _to_wire(name, w):
    if name == "w_in":
        return jnp.pad(w, ((0, 0), (0, 0), (0, IN_WIDTH_PAD - IN_WIDTH)))
    if name == "mla_w_qb":
        return _regroup_qb(w)
    return w


def _from_wire(name, g):
    if name == "w_in":
        return g[:, :, :IN_WIDTH]
    if name == "mla_w_qb":
        return _ungroup_qb(g)
    return g


def _join(name, blocks):
    a = _SHARD_AXIS[name]
    t = jnp.moveaxis(blocks, 0, a)
    return t.reshape(t.shape[:a] + (t.shape[a] * t.shape[a + 1],) + t.shape[a + 2:])


def _cut(name, full):
    a = _SHARD_AXIS[name]
    t = full.reshape(full.shape[:a] + (N_DEV, full.shape[a] // N_DEV) + full.shape[a + 1:])
    return jnp.moveaxis(t, a, 0)


def _heads(t):
    s = t.shape[0]
    return t.reshape(s, N_HEADS, -1).transpose(1, 0, 2)


def _unheads(t):
    return t.transpose(1, 0, 2).reshape(t.shape[1], -1)


def _layer(x, mem, rope, w, p, tag):
    s = x.shape[0]
    cos, sin, cos4, sin4 = rope
    c_q, c_kv, k_rope, sb_qkv, fox_qkv, fox_f, mem_q, gate_z, merge_r = in_proj(x, w["w_in"], IN_SPLITS, tag + "in_proj")

    (cq_n,) = rowwise(_rms_fn, [c_q], [p["mla_q_norm"]], [MLA_Q_RANK], tag + "q_rms")
    q_all = matmul(cq_n, w["mla_w_qb"], tag + "q_up")
    n_nope, n_half = N_HEADS * MLA_NOPE, N_HEADS * MLA_ROPE // 2
    q_nope, q_x1, q_x2 = q_all[:, :n_nope], q_all[:, n_nope:n_nope + n_half], q_all[:, n_nope + n_half:]
    (ckv_n,) = rowwise(_rms_fn, [c_kv], [p["mla_kv_norm"]], [MLA_KV_RANK], tag + "kv_rms")
    kv = matmul(ckv_n, w["mla_w_kvb"], tag + "kv_up").reshape(s, N_HEADS, MLA_NOPE + HEAD_DIM)
    k_nope, v_mla = kv[..., :MLA_NOPE], kv[..., MLA_NOPE:]
    half = MLA_ROPE // 2
    q_o1, q_o2 = rowwise(_rope_fn, [q_x1, q_x2, cos4, sin4], [], [n_half, n_half], tag + "q_rope")
    k_o1, k_o2 = rowwise(_rope_fn, [k_rope[:, :half], k_rope[:, half:], cos, sin], [], [half, half], tag + "k_rope")
    zeros = jnp.zeros((s, N_HEADS, MLA_QK_PAD - MLA_NOPE - MLA_ROPE), F32)
    q_mla = jnp.concatenate([q_nope.reshape(s, N_HEADS, MLA_NOPE), q_o1.reshape(s, N_HEADS, half),
                             q_o2.reshape(s, N_HEADS, half), zeros], axis=-1).transpose(1, 0, 2)
    k_mla = jnp.concatenate([k_nope, jnp.broadcast_to(k_o1[:, None, :], (s, N_HEADS, half)),
                             jnp.broadcast_to(k_o2[:, None, :], (s, N_HEADS, half)), zeros], axis=-1).transpose(1, 0, 2)
    y_mla = softmax_attention(q_mla, k_mla, v_mla.transpose(1, 0, 2), None,
                              (MLA_NOPE + MLA_ROPE) ** -0.5, True, tag + "mla_attn")

    width = N_HEADS * HEAD_DIM
    sq, sk, sv = (_heads(sb_qkv[:, i * width:(i + 1) * width]) for i in range(3))
    y_sb = stick_breaking_attention(sq, sk, sv, HEAD_DIM ** -0.5, tag + "stick_attn")

    fq, fk, fv = (_heads(fox_qkv[:, i * width:(i + 1) * width]) for i in range(3))
    (log_f,) = rowwise(_forget_fn, [fox_f], [p["fox_forget_bias"]], [N_HEADS], tag + "forget_gate")
    log_f8 = jnp.pad(log_f.T, ((0, SUBLANES - N_HEADS), (0, 0)))
    c = cumsum_lanes(log_f8, tag + "forget_cumsum")[:N_HEADS]
    y_fox = softmax_attention(fq, fk, fv, c, HEAD_DIM ** -0.5, True, tag + "fox_attn")

    mkv = matmul(mem, w["w_mem_kv"], tag + "mem_kv")
    y_mem = softmax_attention(_heads(mem_q), _heads(mkv[:, :width]), _heads(mkv[:, width:]), None,
                              HEAD_DIM ** -0.5, False, tag + "mem_attn")

    ys = [_unheads(t) for t in (y_mla, y_sb, y_fox, y_mem)]
    yb = rowwise(_silu_gate_fn, ys + [gate_z], [], [BRANCH_WIDTH] * N_BRANCHES, tag + "silu_gate")
    proj = [matmul(yb[n], w["w_branch"][n], tag + "branch%d" % n) for n in range(N_BRANCHES)]
    merged = merge_gate(merge_r, w["w_merge_up"], proj, tag + "merge")
    out = matmul(merged, w["w_out"], tag + "out_proj")
    (xn,) = rowwise(_deepnorm_fn, [x, out], [p["ln_gain"], p["ln_bias"]], [x.shape[1]], tag + "deepnorm")
    return xn


def _adamw(g, w, m, v, name):
    shape = g.shape
    two_d = (-1, shape[-1])
    g2, w2, m2, v2 = (t.reshape(two_d) for t in (g, w, m, v))
    delta, new_m, new_v = _rowwise_fwd_call(_adamw_fn, [g2, w2, m2, v2], [], [shape[-1]] * 3, name)
    return delta.reshape(shape), new_m.reshape(shape), new_v.reshape(shape)


def kernel(x, mem, positions, w_in, mla_q_norm, mla_w_qb, mla_kv_norm, mla_w_kvb, fox_forget_bias, w_mem_kv, w_merge_up, w_branch, w_out, ln_gain, ln_bias, loss_target, m_w_in, m_mla_q_norm, m_mla_w_qb, m_mla_kv_norm, m_mla_w_kvb, m_fox_forget_bias, m_w_mem_kv, m_w_merge_up, m_w_branch, m_w_out, m_ln_gain, m_ln_bias, v_w_in, v_mla_q_norm, v_mla_w_qb, v_mla_kv_norm, v_mla_w_kvb, v_fox_forget_bias, v_w_mem_kv, v_w_merge_up, v_w_branch, v_w_out, v_ln_gain, v_ln_bias):
    weights = dict(w_in=w_in, mla_q_norm=mla_q_norm, mla_w_qb=mla_w_qb, mla_kv_norm=mla_kv_norm, mla_w_kvb=mla_w_kvb,
                   fox_forget_bias=fox_forget_bias, w_mem_kv=w_mem_kv, w_merge_up=w_merge_up, w_branch=w_branch,
                   w_out=w_out, ln_gain=ln_gain, ln_bias=ln_bias)
    m_in = dict(w_in=m_w_in, mla_q_norm=m_mla_q_norm, mla_w_qb=m_mla_w_qb, mla_kv_norm=m_mla_kv_norm, mla_w_kvb=m_mla_w_kvb,
                fox_forget_bias=m_fox_forget_bias, w_mem_kv=m_w_mem_kv, w_merge_up=m_w_merge_up, w_branch=m_w_branch,
                w_out=m_w_out, ln_gain=m_ln_gain, ln_bias=m_ln_bias)
    v_in = dict(w_in=v_w_in, mla_q_norm=v_mla_q_norm, mla_w_qb=v_mla_w_qb, mla_kv_norm=v_mla_kv_norm, mla_w_kvb=v_mla_w_kvb,
                fox_forget_bias=v_fox_forget_bias, w_mem_kv=v_w_mem_kv, w_merge_up=v_w_merge_up, w_branch=v_w_branch,
                w_out=v_w_out, ln_gain=v_ln_gain, ln_bias=v_ln_bias)
    order = list(weights)
    big = list(_SHARD_AXIS)
    n_layers = w_in.shape[0]
    x0, mem0 = x[0], mem[0]
    s = x0.shape[0]

    wire = [_to_wire(n, weights[n]).astype(WIRE_DTYPE) for n in big]
    sizes = [t.size for t in wire]
    flat = jnp.concatenate([t.reshape(-1) for t in wire]).reshape(-1, LANES)
    gathered = all_gather_hbm(flat, "gather_weights").reshape(N_DEV, -1)
    offs = np.cumsum([0] + sizes)
    full = {n: _join(n, gathered[:, offs[i]:offs[i + 1]].reshape((N_DEV,) + wire[i].shape)) for i, n in enumerate(big)}
    per_layer_w = [{n: full[n][l] for n in big} for l in range(n_layers)]
    per_layer_p = [{n: weights[n][l][None, :] for n in _SMALL} for l in range(n_layers)]

    inv_freq = ROPE_THETA ** (-jnp.arange(0, MLA_ROPE, 2, dtype=F32) / MLA_ROPE)
    ang = positions[0].astype(F32)[:, None] * inv_freq
    cos, sin = jnp.cos(ang), jnp.sin(ang)
    rope = (cos, sin, jnp.tile(cos, (1, N_HEADS)), jnp.tile(sin, (1, N_HEADS)))

    def trunk(x, ws, ps):
        for l in range(n_layers):
            x = _layer(x, mem0, rope, ws[l], ps[l], "l%d_" % l)
        return x

    y, pullback = jax.vjp(trunk, x0, per_layer_w, per_layer_p)
    loss_part, dy = _loss_call(y, loss_target[0])
    grad_x, grad_ws, grad_ps = pullback(dy)

    cut = [_cut(n, jnp.stack([grad_ws[l][n] for l in range(n_layers)])) for n in big]
    blocks = jnp.concatenate([t.reshape(N_DEV, -1) for t in cut], axis=1).reshape(N_DEV, -1, LANES)
    received = all_to_all_hbm(blocks, "scatter_grads")
    summed = sum_blocks(received, "sum_grads").reshape(-1)
    grads = {n: _from_wire(n, summed[offs[i]:offs[i + 1]].reshape(wire[i].shape)) for i, n in enumerate(big)}

    small = [jnp.stack([grad_ps[l][n][0] for l in range(n_layers)]) for n in _SMALL]
    small_sizes = [t.size for t in small]
    packed = jnp.concatenate([t.reshape(-1) for t in small] + [loss_part.reshape(1)])
    rows = -(-packed.size // (SUBLANES * LANES)) * SUBLANES
    packed = jnp.pad(packed, (0, rows * LANES - packed.size)).reshape(rows, LANES)
    reduced = all_reduce_small(packed, "reduce_small").reshape(-1)
    small_offs = np.cumsum([0] + small_sizes)
    for i, n in enumerate(_SMALL):
        grads[n] = reduced[small_offs[i]:small_offs[i + 1]].reshape(small[i].shape)
    loss = reduced[small_offs[-1]]

    delta, new_m, new_v = {}, {}, {}
    for n in order:
        delta[n], new_m[n], new_v[n] = _adamw(grads[n], weights[n], m_in[n], v_in[n], "adamw_" + n)
    return (loss, grad_x[None], *[grads[n] for n in order], *[delta[n] for n in order],
            *[new_m[n] for n in order], *[new_v[n] for n in order])
```

```python
import functools

import numpy as np
import jax
import jax.numpy as jnp
from jax import lax
from jax.experimental import pallas as pl
from jax.experimental.pallas import tpu as pltpu

F32 = jnp.float32
MXU_DTYPE = jnp.bfloat16
WIRE_DTYPE = jnp.bfloat16

N_LAYERS = 4
N_BRANCHES = 4
BRANCH_WIDTH = 256
N_HEADS = 4
HEAD_DIM = 64
MLA_NOPE = 64
MLA_ROPE = 32
MLA_Q_RANK = 384
MLA_KV_RANK = 256
ROPE_THETA = 10000.0
MERGE_RANK = 128
RMS_EPS = 1e-6
LN_EPS = 1e-5
DEEPNORM_ALPHA = (2 * N_LAYERS) ** 0.25
IN_SPLITS = (MLA_Q_RANK, MLA_KV_RANK, MLA_ROPE, 3 * N_HEADS * HEAD_DIM, 3 * N_HEADS * HEAD_DIM,
             N_HEADS, N_HEADS * HEAD_DIM, N_BRANCHES * BRANCH_WIDTH, MERGE_RANK)
IN_WIDTH = sum(IN_SPLITS)
ADAM_LR, ADAM_B1, ADAM_B2, ADAM_EPS, ADAM_WD, ADAM_STEP = 0.001, 0.9, 0.999, 1e-08, 0.01, 10

LANES = 128
SUBLANES = 8
VMEM_BYTES_V7X = 64 * 1024 * 1024
VMEM_LIMIT_CAP = VMEM_BYTES_V7X - 8 * 1024 * 1024
MM_VMEM_BUDGET = 44 * 1024 * 1024
ROW_TILE_BYTES = 8 * 1024 * 1024
ATTN_TILE = 512
SUFFIX_BLOCK = 256
N_DEV = 8
IN_WIDTH_PAD = -(-IN_WIDTH // LANES) * LANES
MLA_QK_PAD = 128


def _cparams(semantics, vmem_estimate):
    limit = int(min(max(2 * vmem_estimate, 32 * 1024 * 1024), VMEM_LIMIT_CAP))
    return pltpu.CompilerParams(dimension_semantics=semantics, vmem_limit_bytes=limit)


def _dot(a, b, ca=1, cb=0):
    return lax.dot_general(a, b, (((ca,), (cb,)), ((), ())), preferred_element_type=F32)


def _div_tile(n, cap, align):
    t = (cap // align) * align
    while t >= align:
        if n % t == 0:
            return t
        t -= align
    return n


def _mm_call(a, b, ta, tb, out_dtype, name):
    k, m = a.shape if ta else a.shape[::-1]
    n, kb = b.shape if tb else b.shape[::-1]
    assert k == kb, (a.shape, b.shape, ta, tb)
    ab, bb, ob = a.dtype.itemsize, b.dtype.itemsize, jnp.dtype(out_dtype).itemsize
    tk = k if k <= 4096 else 512
    tn = n if n <= 4096 else _div_tile(n, 2048, LANES)
    cands = [t for t in (1024, 512, 256, 128) if t <= m and m % t == 0] or [m]
    for tm in cands:
        est = 2 * (tm * tk * ab + tk * tn * bb + tm * tn * ob) + (tm * tn * 4 if k > tk else 0)
        if est <= MM_VMEM_BUDGET:
            break
    nk = k // tk

    def body(a_ref, b_ref, o_ref, *scratch):
        p = _dot(a_ref[...].astype(MXU_DTYPE), b_ref[...].astype(MXU_DTYPE), 0 if ta else 1, 1 if tb else 0)
        if nk == 1:
            o_ref[...] = p.astype(o_ref.dtype)
        else:
            acc = scratch[0]
            kk = pl.program_id(2)

            @pl.when(kk == 0)
            def _():
                acc[...] = p

            @pl.when(kk > 0)
            def _():
                acc[...] += p

            @pl.when(kk == nk - 1)
            def _():
                o_ref[...] = acc[...].astype(o_ref.dtype)

    a_spec = pl.BlockSpec((tk, tm), lambda i, j, l: (l, i)) if ta else pl.BlockSpec((tm, tk), lambda i, j, l: (i, l))
    b_spec = pl.BlockSpec((tn, tk), lambda i, j, l: (j, l)) if tb else pl.BlockSpec((tk, tn), lambda i, j, l: (l, j))
    return pl.pallas_call(
        body, name=name,
        out_shape=jax.ShapeDtypeStruct((m, n), out_dtype),
        grid=(m // tm, n // tn, nk),
        in_specs=[a_spec, b_spec],
        out_specs=pl.BlockSpec((tm, tn), lambda i, j, l: (i, j)),
        scratch_shapes=[pltpu.VMEM((tm, tn), F32)] if nk > 1 else [],
        compiler_params=_cparams(("parallel", "parallel", "arbitrary"), est),
    )(a, b)


def matmul(a, b, name):
    @jax.custom_vjp
    def op(a, b):
        return _mm_call(a, b, False, False, F32, name)

    def fwd(a, b):
        return op(a, b), (a, b)

    def bwd(res, g):
        a, b = res
        da = _mm_call(g, b, False, True, a.dtype, name + "_da")
        db = _mm_call(a, g, True, False, b.dtype, name + "_db")
        return da, db

    op.defvjp(fwd, bwd)
    return op(a, b)


def _row_tile(n_rows, bytes_per_row):
    align = 2 * SUBLANES
    cap = max(align, ROW_TILE_BYTES // bytes_per_row)
    best = 0
    for t in range(align, min(n_rows, cap, 2048) + 1, align):
        if n_rows % t == 0:
            best = t
    return best or n_rows


def _row_spec(ts, cols):
    return pl.BlockSpec((ts, cols), lambda i: (i, 0))


def _full_spec(shape):
    return pl.BlockSpec(shape, lambda i: (0,) * len(shape))


def _rowwise_fwd_call(f, rows, params, out_cols, name):
    n = rows[0].shape[0]
    per_row = 4 * (sum(r.shape[1] for r in rows) + sum(out_cols))
    ts = _row_tile(n, per_row)
    n_in, n_p = len(rows), len(params)

    def body(*refs):
        ins = [r[...] for r in refs[:n_in + n_p]]
        outs = f(*ins)
        for o_ref, o in zip(refs[n_in + n_p:], outs):
            o_ref[...] = o.astype(o_ref.dtype)

    return pl.pallas_call(
        body, name=name,
        out_shape=[jax.ShapeDtypeStruct((n, c), F32) for c in out_cols],
        grid=(n // ts,),
        in_specs=[_row_spec(ts, r.shape[1]) for r in rows] + [_full_spec(p.shape) for p in params],
        out_specs=[_row_spec(ts, c) for c in out_cols],
        compiler_params=_cparams(("parallel",), 2 * ts * per_row),
    )(*rows, *params)


def _rowwise_bwd_call(f, rows, params, cts, name):
    n = rows[0].shape[0]
    per_row = 4 * (2 * sum(r.shape[1] for r in rows) + 2 * sum(c.shape[1] for c in cts))
    ts = _row_tile(n, per_row)
    n_in, n_p, n_ct = len(rows), len(params), len(cts)

    def body(*refs):
        ins = [r[...] for r in refs[:n_in + n_p]]
        ct = tuple(r[...] for r in refs[n_in + n_p:n_in + n_p + n_ct])
        grads = jax.vjp(lambda *a: tuple(f(*a)), *ins)[1](ct)
        out_refs = refs[n_in + n_p + n_ct:]
        for o_ref, g in zip(out_refs[:n_in], grads[:n_in]):
            o_ref[...] = g
        first = pl.program_id(0) == 0
        for o_ref, g in zip(out_refs[n_in:], grads[n_in:]):
            @pl.when(first)
            def _(o_ref=o_ref, g=g):
                o_ref[...] = g

            @pl.when(jnp.logical_not(first))
            def _(o_ref=o_ref, g=g):
                o_ref[...] += g

    outs = pl.pallas_call(
        body, name=name,
        out_shape=[jax.ShapeDtypeStruct(r.shape, F32) for r in rows] + [jax.ShapeDtypeStruct(p.shape, F32) for p in params],
        grid=(n // ts,),
        in_specs=([_row_spec(ts, r.shape[1]) for r in rows] + [_full_spec(p.shape) for p in params]
                  + [_row_spec(ts, c.shape[1]) for c in cts]),
        out_specs=[_row_spec(ts, r.shape[1]) for r in rows] + [_full_spec(p.shape) for p in params],
        compiler_params=_cparams(("arbitrary",), 2 * ts * per_row),
    )(*rows, *params, *cts)
    return outs[:n_in], outs[n_in:]


def rowwise(f, rows, params, out_cols, name):
    @jax.custom_vjp
    def op(rows, params):
        return tuple(_rowwise_fwd_call(f, rows, params, out_cols, name))

    def fwd(rows, params):
        return op(rows, params), (rows, params)

    def bwd(res, cts):
        rows, params = res
        d_rows, d_params = _rowwise_bwd_call(f, rows, params, cts, name + "_bwd")
        return tuple(d_rows), tuple(d_params)

    op.defvjp(fwd, bwd)
    return op(tuple(rows), tuple(params))


def _rms_fn(x, g):
    return (x * lax.rsqrt(jnp.mean(x * x, axis=-1, keepdims=True) + RMS_EPS) * g,)


def _rope_fn(x1, x2, cos, sin):
    return x1 * cos - x2 * sin, x1 * sin + x2 * cos


def _log_sigmoid(x):
    return jnp.minimum(x, 0.0) - jnp.log1p(jnp.exp(-jnp.abs(x)))


def _forget_fn(f, bias):
    return (_log_sigmoid(f + bias),)


def _silu_gate_fn(y0, y1, y2, y3, gate_z):
    return tuple(y * (z * jax.nn.sigmoid(z))
                 for y, z in zip((y0, y1, y2, y3), (gate_z[:, n * BRANCH_WIDTH:(n + 1) * BRANCH_WIDTH] for n in range(N_BRANCHES))))


def _deepnorm_fn(x, out, g, b):
    h = DEEPNORM_ALPHA * x + out
    mu = jnp.mean(h, axis=-1, keepdims=True)
    var = jnp.mean(jnp.square(h - mu), axis=-1, keepdims=True)
    return ((h - mu) * lax.rsqrt(var + LN_EPS) * g + b,)


def _adamw_fn(g, w, m, v):
    m = ADAM_B1 * m + (1.0 - ADAM_B1) * g
    v = ADAM_B2 * v + (1.0 - ADAM_B2) * jnp.square(g)
    m_hat = m / (1.0 - ADAM_B1 ** ADAM_STEP)
    v_hat = v / (1.0 - ADAM_B2 ** ADAM_STEP)
    delta = -ADAM_LR * (m_hat / (jnp.sqrt(v_hat) + ADAM_EPS) + ADAM_WD * w)
    return delta, m, v


def in_proj(x, w, sizes, name):
    offs = np.cumsum((0,) + tuple(sizes))
    pad = w.shape[1] - int(offs[-1])

    @jax.custom_vjp
    def op(x, w):
        h = _mm_call(x, w, False, False, F32, name)
        return tuple(h[:, offs[i]:offs[i + 1]] for i in range(len(sizes)))

    def fwd(x, w):
        return op(x, w), (x, w)

    def bwd(res, cts):
        x, w = res
        parts = [c.astype(MXU_DTYPE) for c in cts] + ([jnp.zeros((x.shape[0], pad), MXU_DTYPE)] if pad else [])
        dh = jnp.concatenate(parts, axis=1)
        return _mm_call(dh, w, False, True, x.dtype, name + "_da"), _mm_call(x, dh, True, False, w.dtype, name + "_db")

    op.defvjp(fwd, bwd)
    return op(x, w)


MERGE_ROWS = 256


def _merge_fwd_call(r, w_up, ps, name):
    n, rank = r.shape
    d = ps[0].shape[1]
    nb = len(ps)
    ts = min(MERGE_ROWS, n)

    def body(r_ref, w_ref, *refs):
        rb = r_ref[...].astype(MXU_DTYPE)
        acc = jnp.zeros((ts, d), F32)
        for i in range(nb):
            gate = jax.nn.sigmoid(_dot(rb, w_ref[:, i * d:(i + 1) * d]))
            acc = acc + gate * refs[i][...]
        refs[nb][...] = acc

    return pl.pallas_call(
        body, name=name,
        out_shape=jax.ShapeDtypeStruct((n, d), F32),
        grid=(n // ts,),
        in_specs=[_row_spec(ts, rank), _full_spec(w_up.shape)] + [_row_spec(ts, d)] * nb,
        out_specs=_row_spec(ts, d),
        compiler_params=_cparams(("parallel",), 2 * 4 * ts * d * (nb + 2)),
    )(r, w_up, *ps)


def _merge_bwd_call(r, w_up, ps, dm, name):
    n, rank = r.shape
    d = ps[0].shape[1]
    nb = len(ps)
    ts = min(MERGE_ROWS, n)
    steps = n // ts

    def body(r_ref, w_ref, *refs):
        p_refs, dm_ref = refs[:nb], refs[nb]
        dr_ref, dw_ref, dp_refs, dw_acc = refs[nb + 1], refs[nb + 2], refs[nb + 3:2 * nb + 3], refs[2 * nb + 3]
        step = pl.program_id(0)

        @pl.when(step == 0)
        def _():
            dw_acc[...] = jnp.zeros_like(dw_acc)

        rb = r_ref[...].astype(MXU_DTYPE)
        dmv = dm_ref[...]
        dr = jnp.zeros((ts, rank), F32)
        for i in range(nb):
            w_i = w_ref[:, i * d:(i + 1) * d]
            gate = jax.nn.sigmoid(_dot(rb, w_i))
            dp_refs[i][...] = dmv * gate
            d_logit = (dmv * p_refs[i][...] * gate * (1.0 - gate)).astype(MXU_DTYPE)
            dr = dr + _dot(d_logit, w_i, 1, 1)
            dw_acc[:, i * d:(i + 1) * d] += _dot(rb, d_logit, 0, 0)
        dr_ref[...] = dr

        @pl.when(step == steps - 1)
        def _():
            dw_ref[...] = dw_acc[...].astype(dw_ref.dtype)

    outs = pl.pallas_call(
        body, name=name,
        out_shape=[jax.ShapeDtypeStruct((n, rank), F32), jax.ShapeDtypeStruct(w_up.shape, w_up.dtype)]
        + [jax.ShapeDtypeStruct((n, d), F32)] * nb,
        grid=(steps,),
        in_specs=[_row_spec(ts, rank), _full_spec(w_up.shape)] + [_row_spec(ts, d)] * (nb + 1),
        out_specs=[_row_spec(ts, rank), _full_spec(w_up.shape)] + [_row_spec(ts, d)] * nb,
        scratch_shapes=[pltpu.VMEM(w_up.shape, F32)],
        compiler_params=_cparams(("arbitrary",), 2 * 4 * ts * d * (2 * nb + 2) + 8 * w_up.size),
    )(r, w_up, *ps, dm)
    return outs[0], outs[1], outs[2:]


def merge_gate(r, w_up, ps, name):
    @jax.custom_vjp
    def op(r, w_up, ps):
        return _merge_fwd_call(r, w_up, ps, name)

    def fwd(r, w_up, ps):
        return op(r, w_up, ps), (r, w_up, ps)

    def bwd(res, dm):
        r, w_up, ps = res
        dr, dw, dps = _merge_bwd_call(r, w_up, ps, dm, name + "_bwd")
        return dr, dw, tuple(dps)

    op.defvjp(fwd, bwd)
    return op(r, w_up, tuple(ps))


LOG2E = 1.4426950408889634


def _to_lanes(t):
    return t.transpose(0, 2, 1)


def _key_tiles(t, tk):
    h, s, d = t.shape
    return t.reshape(h, s // tk, tk, d).transpose(0, 1, 3, 2)


def _attn_vmem(sk, dq, dv, tq, tk, backward):
    resident = 2 * 2 * sk * (dq + dv) * (2 if backward else 1)
    grads = 2 * 4 * sk * (dq + dv) if backward else 0
    return resident + grads + 24 * tq * tk * 4


def _tile_ids(q0, k0, tq, tk):
    key = k0 + lax.broadcasted_iota(jnp.int32, (tk, tq), 0)
    qry = q0 + lax.broadcasted_iota(jnp.int32, (tk, tq), 1)
    return key, qry


def _lane_spec(d, tq):
    return pl.BlockSpec((1, d, tq), lambda hh, i: (hh, 0, i))


def _row_major_spec(tq, d):
    return pl.BlockSpec((1, tq, d), lambda hh, i: (hh, i, 0))


def _resident_spec(shape):
    return pl.BlockSpec((1,) + shape, lambda hh, i: (hh,) + (0,) * len(shape))


def _softmax_fwd_call(qt, k, vt, causal, name):
    h, dq, s = qt.shape
    sk, (n_kb, dv, tk) = k.shape[1], vt.shape[1:]
    tq = min(ATTN_TILE, s)
    assert not causal or (tq == tk and s == sk)

    def body(q_ref, k_ref, v_ref, o_ref, lse_ref):
        qi = pl.program_id(1)
        q_t = q_ref[0]

        def tile(kb, carry, masked):
            m, l, acc = carry
            off = pl.multiple_of(kb * tk, tk)
            st = _dot(k_ref[0, pl.ds(off, tk), :], q_t)
            if masked:
                key, qry = _tile_ids(qi * tq, kb * tk, tq, tk)
                st = jnp.where(key <= qry, st, -jnp.inf)
            m_new = jnp.maximum(m, jnp.max(st, axis=0, keepdims=True))
            alpha = jnp.exp2(m - m_new)
            p = jnp.exp2(st - m_new)
            l = alpha * l + jnp.sum(p, axis=0, keepdims=True)
            acc = alpha * acc + _dot(v_ref[0, kb], p.astype(MXU_DTYPE))
            return m_new, l, acc

        def pair(i, carry):
            m, l, acc = carry
            off = pl.multiple_of(2 * i * tk, 2 * tk)
            st = _dot(k_ref[0, pl.ds(off, 2 * tk), :], q_t)
            m_new = jnp.maximum(m, jnp.max(st, axis=0, keepdims=True))
            alpha = jnp.exp2(m - m_new)
            p = jnp.exp2(st - m_new)
            l = alpha * l + jnp.sum(p, axis=0, keepdims=True)
            p = p.astype(MXU_DTYPE)
            acc = alpha * acc + _dot(v_ref[0, 2 * i], p[:tk]) + _dot(v_ref[0, 2 * i + 1], p[tk:])
            return m_new, l, acc

        carry = (jnp.full((1, tq), -jnp.inf, F32), jnp.zeros((1, tq), F32), jnp.zeros((dv, tq), F32))
        n_full = qi if causal else n_kb
        if n_kb >= 2:
            carry = lax.fori_loop(0, n_full // 2, pair, carry)
        if causal:
            carry = lax.cond(n_full % 2 == 1, lambda c: tile(n_full - 1, c, False), lambda c: c, carry)
            carry = tile(qi, carry, True)
        elif n_kb % 2:
            carry = tile(n_kb - 1, carry, False)
        m, l, acc = carry
        o_ref[0] = acc / l
        lse_ref[0] = m + jnp.log2(l)

    return pl.pallas_call(
        body, name=name,
        out_shape=[jax.ShapeDtypeStruct((h, dv, s), F32), jax.ShapeDtypeStruct((h, 1, s), F32)],
        grid=(h, s // tq),
        in_specs=[_lane_spec(dq, tq), _resident_spec((sk, dq)), _resident_spec((n_kb, dv, tk))],
        out_specs=[_lane_spec(dv, tq), _lane_spec(1, tq)],
        compiler_params=_cparams(("parallel", "arbitrary"), _attn_vmem(sk, dq, dv, tq, tk, False)),
    )(qt, k, vt)


def _softmax_bwd_call(qt, q, k, kt, v, ot, lse, dot, do, causal, name):
    h, dq, s = qt.shape
    sk, dv = k.shape[1], v.shape[2]
    n_kb, tk = kt.shape[1], kt.shape[3]
    tq = min(ATTN_TILE, s)

    def body(qt_ref, q_ref, k_ref, kt_ref, v_ref, ot_ref, lse_ref, dot_ref, do_ref, gq_ref, gk_ref, gv_ref):
        qi = pl.program_id(1)

        @pl.when(qi == 0)
        def _():
            gk_ref[...] = jnp.zeros_like(gk_ref)
            gv_ref[...] = jnp.zeros_like(gv_ref)

        q_t, q_n, do_n = qt_ref[0], q_ref[0], do_ref[0]
        do_t = dot_ref[0]
        do_tb = do_t.astype(MXU_DTYPE)
        delta = jnp.sum(do_t * ot_ref[0], axis=0, keepdims=True)
        lse_q = lse_ref[0]

        def tile(kb, gq_acc, masked):
            off = pl.multiple_of(kb * tk, tk)
            st = _dot(k_ref[0, pl.ds(off, tk), :], q_t)
            if masked:
                key, qry = _tile_ids(qi * tq, kb * tk, tq, tk)
                st = jnp.where(key <= qry, st, -jnp.inf)
            p = jnp.exp2(st - lse_q)
            dp = _dot(v_ref[0, pl.ds(off, tk), :], do_tb)
            ds = (p * (dp - delta)).astype(MXU_DTYPE)
            gk_ref[0, pl.ds(off, tk), :] += _dot(ds, q_n)
            gv_ref[0, pl.ds(off, tk), :] += _dot(p.astype(MXU_DTYPE), do_n)
            return gq_acc + _dot(kt_ref[0, kb], ds)

        gq = lax.fori_loop(0, qi if causal else n_kb, lambda kb, c: tile(kb, c, False), jnp.zeros((dq, tq), F32))
        if causal:
            gq = tile(qi, gq, True)
        gq_ref[0] = gq

    return pl.pallas_call(
        body, name=name,
        out_shape=[jax.ShapeDtypeStruct((h, dq, s), F32), jax.ShapeDtypeStruct((h, sk, dq), F32), jax.ShapeDtypeStruct((h, sk, dv), F32)],
        grid=(h, s // tq),
        in_specs=[_lane_spec(dq, tq), _row_major_spec(tq, dq), _resident_spec((sk, dq)), _resident_spec((n_kb, dq, tk)),
                  _resident_spec((sk, dv)), _lane_spec(dv, tq), _lane_spec(1, tq), _lane_spec(dv, tq), _row_major_spec(tq, dv)],
        out_specs=[_lane_spec(dq, tq), _resident_spec((sk, dq)), _resident_spec((sk, dv))],
        compiler_params=_cparams(("parallel", "arbitrary"), _attn_vmem(sk, dq, dv, tq, tk, True)),
    )(qt, q, k, kt, v, ot, lse, dot, do)


def _split_terms(x, n):
    info = jnp.finfo(MXU_DTYPE)
    terms = []
    for _ in range(n):
        t = lax.reduce_precision(x, info.nexp, info.nmant)
        terms.append(t)
        x = x - t
    return terms


def softmax_attention(q, k, v, c, scale, causal, name):
    decay = c is not None
    d = q.shape[2]
    tk = min(ATTN_TILE, k.shape[1])

    def widen(q, k, c):
        qs = q * (scale * LOG2E)
        if decay:
            terms = jnp.stack(_split_terms(c * LOG2E, 3), axis=-1)
            ones = jnp.ones_like(terms)
            pad = jnp.zeros(q.shape[:2] + (LANES - d - 6,), F32)
            qs = jnp.concatenate([qs, terms, ones, pad], axis=-1)
            k = jnp.concatenate([k, ones, -terms, pad], axis=-1)
        return qs.astype(MXU_DTYPE), k.astype(MXU_DTYPE)

    def run(q, k, v, c):
        qa, ka = widen(q, k, c)
        vb = v.astype(MXU_DTYPE)
        qt = _to_lanes(qa)
        ot, lse = _softmax_fwd_call(qt, ka, _key_tiles(vb, tk), causal, name)
        return _to_lanes(ot), (qt, qa, ka, vb, ot, lse)

    @jax.custom_vjp
    def op(q, k, v, c):
        return run(q, k, v, c)[0]

    def bwd(res, do):
        qt, qa, ka, vb, ot, lse = res
        gqt, gk, gv = _softmax_bwd_call(qt, qa, ka, _key_tiles(ka, tk), vb, ot, lse, _to_lanes(do),
                                        do.astype(MXU_DTYPE), causal, name + "_bwd")
        gq = _to_lanes(gqt)
        dc = gq[..., d] - gk[..., d + 3] if decay else None
        return gq[..., :d] * scale, gk[..., :d] * (1.0 / LOG2E), gv, dc

    op.defvjp(run, bwd)
    return op(q, k, v, c)


def _running_sum(ones, x, suffix):
    xb = x.astype(MXU_DTYPE)
    b = ones.shape[0]
    n = x.shape[0] // b
    out, carry = [None] * n, None
    for i in (reversed(range(n)) if suffix else range(n)):
        blk = _dot(ones, xb[i * b:(i + 1) * b])
        blk = blk if carry is None else blk + carry
        carry = blk[0:1, :] if suffix else blk[b - 1:b, :]
        out[i] = blk
    return out[0] if n == 1 else jnp.concatenate(out, axis=0)


def _ones_after(tk):
    b = min(tk, SUFFIX_BLOCK)
    return (lax.broadcasted_iota(jnp.int32, (b, b), 1) >= lax.broadcasted_iota(jnp.int32, (b, b), 0)).astype(MXU_DTYPE)


def _ones_before(tk):
    b = min(tk, SUFFIX_BLOCK)
    return (lax.broadcasted_iota(jnp.int32, (b, b), 1) <= lax.broadcasted_iota(jnp.int32, (b, b), 0)).astype(MXU_DTYPE)


LOGIT_CLAMP = 120.0


def _log_keep(k_tile, q_t, strict):
    z = jnp.minimum(_dot(k_tile, q_t), LOGIT_CLAMP)
    log_keep = -(jnp.log(1.0 + jnp.exp2(z)) * LOG2E)
    return z, (log_keep if strict is None else jnp.where(strict, log_keep, 0.0))


def _stick_weights(z, log_keep, later, ones_after, strict):
    local = _running_sum(ones_after, log_keep, True)
    a = jnp.exp2(z + local + later)
    return (a if strict is None else jnp.where(strict, a, 0.0)), local


def _strict_mask(qi, kb, t):
    key, qry = _tile_ids(qi * t, kb * t, t, t)
    return key < qry


def _stick_fwd_call(qt, k, vt, name):
    h, d, s = qt.shape
    t = vt.shape[3]
    n_t = s // t
    assert t == min(ATTN_TILE, s)

    def body(q_ref, k_ref, v_ref, o_ref, later_ref):
        qi = pl.program_id(1)
        q_t = q_ref[0]
        ones_after = _ones_after(t)
        later_ref[...] = jnp.zeros_like(later_ref)

        def tile(kb, carry, strict):
            later, acc = carry
            off = pl.multiple_of(kb * t, t)
            z, log_keep = _log_keep(k_ref[0, pl.ds(off, t), :], q_t, strict)
            a, local = _stick_weights(z, log_keep, later, ones_after, strict)
            later_ref[0, kb] = later
            return later + local[0:1, :], acc + _dot(v_ref[0, kb], a.astype(MXU_DTYPE))

        def pair(i, carry):
            later, acc = carry
            right = qi - 1 - 2 * i
            off = pl.multiple_of((right - 1) * t, t)
            z, log_keep = _log_keep(k_ref[0, pl.ds(off, 2 * t), :], q_t, None)
            a, local = _stick_weights(z, log_keep, later, ones_after, None)
            later_ref[0, right] = later
            later_ref[0, right - 1] = later + local[t:t + 1, :]
            a = a.astype(MXU_DTYPE)
            return later + local[0:1, :], acc + _dot(v_ref[0, right - 1], a[:t]) + _dot(v_ref[0, right], a[t:])

        carry = tile(qi, (jnp.zeros((1, t), F32), jnp.zeros((d, t), F32)), _strict_mask(qi, qi, t))
        if n_t >= 2:
            carry = lax.fori_loop(0, qi // 2, pair, carry)
        _, acc = lax.cond(qi % 2 == 1, lambda c: tile(0, c, None), lambda c: c, carry)
        o_ref[0] = acc

    later_spec = pl.BlockSpec((1, n_t, 1, t), lambda hh, i: (hh, 0, 0, i))
    return pl.pallas_call(
        body, name=name,
        out_shape=[jax.ShapeDtypeStruct((h, d, s), F32), jax.ShapeDtypeStruct((h, n_t, 1, s), F32)],
        grid=(h, n_t),
        in_specs=[_lane_spec(d, t), _resident_spec((s, d)), _resident_spec((n_t, d, t))],
        out_specs=[_lane_spec(d, t), later_spec],
        compiler_params=_cparams(("parallel", "arbitrary"), _attn_vmem(s, d, d, t, t, False)),
    )(qt, k, vt)


def _stick_bwd_call(qt, q, k, kt, v, later, dot, do, name):
    h, d, s = qt.shape
    t = kt.shape[3]
    n_t = s // t

    def body(qt_ref, q_ref, k_ref, kt_ref, v_ref, later_ref, dot_ref, do_ref, gq_ref, gk_ref, gv_ref):
        qi = pl.program_id(1)

        @pl.when(qi == 0)
        def _():
            gk_ref[...] = jnp.zeros_like(gk_ref)
            gv_ref[...] = jnp.zeros_like(gv_ref)

        q_t, q_n, do_n = qt_ref[0], q_ref[0], do_ref[0]
        do_tb = dot_ref[0].astype(MXU_DTYPE)
        ones_after, ones_before = _ones_after(t), _ones_before(t)
        diagonal = _strict_mask(qi, qi, t)

        def sweep_right(kb, carry, strict):
            before, gq_acc = carry
            off = pl.multiple_of(kb * t, t)
            z, log_keep = _log_keep(k_ref[0, pl.ds(off, t), :], q_t, strict)
            a, _ = _stick_weights(z, log_keep, later_ref[0, kb], ones_after, strict)
            g = a * _dot(v_ref[0, pl.ds(off, t), :], do_tb)
            g_prefix = _running_sum(ones_before, g, False) + before
            sig = 1.0 - jnp.exp2(log_keep)
            dz = g - sig * g_prefix
            dz = (dz if strict is None else jnp.where(strict, dz, 0.0)).astype(MXU_DTYPE)
            gk_ref[0, pl.ds(off, t), :] += _dot(dz, q_n)
            gv_ref[0, pl.ds(off, t), :] += _dot(a.astype(MXU_DTYPE), do_n)
            return g_prefix[t - 1:t, :], gq_acc + _dot(kt_ref[0, kb], dz)

        carry = lax.fori_loop(0, qi, lambda kb, c: sweep_right(kb, c, None), (jnp.zeros((1, t), F32), jnp.zeros((d, t), F32)))
        _, gq = sweep_right(qi, carry, diagonal)
        gq_ref[0] = gq

    return pl.pallas_call(
        body, name=name,
        out_shape=[jax.ShapeDtypeStruct((h, d, s), F32), jax.ShapeDtypeStruct((h, s, d), F32), jax.ShapeDtypeStruct((h, s, d), F32)],
        grid=(h, n_t),
        in_specs=[_lane_spec(d, t), _row_major_spec(t, d), _resident_spec((s, d)), _resident_spec((n_t, d, t)),
                  _resident_spec((s, d)), pl.BlockSpec((1, n_t, 1, t), lambda hh, i: (hh, 0, 0, i)),
                  _lane_spec(d, t), _row_major_spec(t, d)],
        out_specs=[_lane_spec(d, t), _resident_spec((s, d)), _resident_spec((s, d))],
        compiler_params=_cparams(("parallel", "arbitrary"), _attn_vmem(s, d, d, t, t, True)),
    )(qt, q, k, kt, v, later, dot, do)


def stick_breaking_attention(q, k, v, scale, name):
    t = min(ATTN_TILE, q.shape[1])

    def run(q, k, v):
        qb, kb, vb = (q * (scale * LOG2E)).astype(MXU_DTYPE), k.astype(MXU_DTYPE), v.astype(MXU_DTYPE)
        qt = _to_lanes(qb)
        ot, later = _stick_fwd_call(qt, kb, _key_tiles(vb, t), name)
        return _to_lanes(ot), (qt, qb, kb, vb, later)

    @jax.custom_vjp
    def op(q, k, v):
        return run(q, k, v)[0]

    def bwd(res, do):
        qt, qb, kb, vb, later = res
        gqt, gk, gv = _stick_bwd_call(qt, qb, kb, _key_tiles(kb, t), vb, later, _to_lanes(do), do.astype(MXU_DTYPE), name + "_bwd")
        return _to_lanes(gqt) * scale, gk * (1.0 / LOG2E), gv

    op.defvjp(run, bwd)
    return op(q, k, v)


def _cumsum_call(x, reverse, name):
    r, s = x.shape
    tb = min(ATTN_TILE, s)
    nb = s // tb

    def body(x_ref, o_ref, carry):
        @pl.when(pl.program_id(0) == 0)
        def _():
            carry[...] = jnp.zeros_like(carry)

        j = lax.broadcasted_iota(jnp.int32, (tb, tb), 0)
        t = lax.broadcasted_iota(jnp.int32, (tb, tb), 1)
        ones = ((j >= t) if reverse else (j <= t)).astype(MXU_DTYPE)
        rest = x_ref[...]
        y = carry[:, 0:1]
        for _ in range(3):
            term = rest.astype(MXU_DTYPE)
            rest = rest - term.astype(F32)
            y = y + _dot(term, ones)
        o_ref[...] = y
        carry[...] = jnp.broadcast_to(y[:, 0:1] if reverse else y[:, tb - 1:tb], carry.shape)

    spec = pl.BlockSpec((r, tb), (lambda i: (0, nb - 1 - i)) if reverse else (lambda i: (0, i)))
    return pl.pallas_call(
        body, name=name,
        out_shape=jax.ShapeDtypeStruct((r, s), F32),
        grid=(nb,),
        in_specs=[spec], out_specs=spec,
        scratch_shapes=[pltpu.VMEM((r, LANES), F32)],
        compiler_params=_cparams(("arbitrary",), 1024 * 1024),
    )(x)


def cumsum_lanes(x, name):
    @jax.custom_vjp
    def op(x):
        return _cumsum_call(x, False, name)

    op.defvjp(lambda x: (op(x), None), lambda _, g: (_cumsum_call(g, True, name + "_bwd"),))
    return op(x)


def _loss_call(y, target):
    n, d = y.shape
    ts = _row_tile(n, 4 * 3 * d)

    def body(y_ref, t_ref, dy_ref, loss_ref):
        err = y_ref[...] - t_ref[...]
        dy_ref[...] = err * (1.0 / d)
        part = jnp.sum(jnp.sum(err * err, axis=1, keepdims=True), axis=0, keepdims=True) * (0.5 / d)

        @pl.when(pl.program_id(0) == 0)
        def _():
            loss_ref[...] = jnp.zeros_like(loss_ref)

        loss_ref[...] += jnp.broadcast_to(part, loss_ref.shape)

    dy, loss = pl.pallas_call(
        body, name="loss_head",
        out_shape=[jax.ShapeDtypeStruct((n, d), F32), jax.ShapeDtypeStruct((SUBLANES, LANES), F32)],
        grid=(n // ts,),
        in_specs=[_row_spec(ts, d), _row_spec(ts, d)],
        out_specs=[_row_spec(ts, d), _full_spec((SUBLANES, LANES))],
        compiler_params=_cparams(("arbitrary",), 2 * ts * 4 * 3 * d),
    )(y, target)
    return loss[0, 0], dy


_FLIPS = [(0, 0, 1), (0, 1, 0), (0, 1, 1), (1, 0, 0), (1, 0, 1), (1, 1, 0), (1, 1, 1)]
MESH = pl.DeviceIdType.MESH
ANY_SPEC = pl.BlockSpec(memory_space=pl.ANY)


def _me_and_peers():
    x, y, c = lax.axis_index("x"), lax.axis_index("y"), lax.axis_index("c")
    peers = [((1 - x) if fx else x, (1 - y) if fy else y, (1 - c) if fc else c) for fx, fy, fc in _FLIPS]
    return 4 * x + 2 * y + c, peers, [4 * px + 2 * py + pc for px, py, pc in peers]


def all_gather_hbm(shard, name):
    def body(x_ref, out_ref, send_sems, recv_sems, local_sem):
        x, y, c = lax.axis_index("x"), lax.axis_index("y"), lax.axis_index("c")
        me, sibling = (x, y, c), (x, y, 1 - c)
        chips = [(1 - x, y), (x, 1 - y), (1 - x, 1 - y)]

        def block(px, py, pc):
            return out_ref.at[4 * px + 2 * py + pc]

        def copy(i, blk, to, src=None):
            return pltpu.make_async_remote_copy(src_ref=block(*blk) if src is None else src, dst_ref=block(*blk),
                                                send_sem=send_sems.at[i], recv_sem=recv_sems.at[i],
                                                device_id=to, device_id_type=MESH)

        mine = pltpu.make_async_copy(x_ref, block(*me), local_sem)
        mine.start()
        first = [copy(0, me, sibling, src=x_ref)] + [copy(1 + j, me, (*chip, c), src=x_ref) for j, chip in enumerate(chips)]
        for cp in first:
            cp.start()
        passed = [copy(4 + j, (*chip, c), sibling) for j, chip in enumerate(chips)]
        for j, chip in enumerate(chips):
            copy(1 + j, (*chip, c), me).wait_recv()
            passed[j].start()
        copy(0, sibling, me).wait_recv()
        for j, chip in enumerate(chips):
            copy(4 + j, (*chip, 1 - c), me).wait_recv()
        for cp in first + passed:
            cp.wait_send()
        mine.wait()

    return pl.pallas_call(
        body, name=name,
        out_shape=jax.ShapeDtypeStruct((N_DEV,) + shard.shape, shard.dtype),
        in_specs=[ANY_SPEC], out_specs=ANY_SPEC,
        scratch_shapes=[pltpu.SemaphoreType.DMA((N_DEV - 1,)), pltpu.SemaphoreType.DMA((N_DEV - 1,)), pltpu.SemaphoreType.DMA],
    )(shard)


def all_to_all_hbm(blocks, name):
    def body(x_ref, out_ref, send_sems, recv_sems, local_sem):
        me, peers, peer_ids = _me_and_peers()
        mine = pltpu.make_async_copy(x_ref.at[me], out_ref.at[me], local_sem)
        mine.start()
        sends = [pltpu.make_async_remote_copy(src_ref=x_ref.at[pid], dst_ref=out_ref.at[me], send_sem=send_sems.at[i],
                                              recv_sem=recv_sems.at[i], device_id=p, device_id_type=MESH)
                 for i, (p, pid) in enumerate(zip(peers, peer_ids))]
        for cp in sends:
            cp.start()
        for i, (p, pid) in enumerate(zip(peers, peer_ids)):
            pltpu.make_async_remote_copy(src_ref=x_ref.at[me], dst_ref=out_ref.at[pid], send_sem=send_sems.at[i],
                                         recv_sem=recv_sems.at[i], device_id=p, device_id_type=MESH).wait_recv()
        for cp in sends:
            cp.wait_send()
        mine.wait()

    return pl.pallas_call(
        body, name=name,
        out_shape=jax.ShapeDtypeStruct(blocks.shape, blocks.dtype),
        in_specs=[ANY_SPEC], out_specs=ANY_SPEC,
        scratch_shapes=[pltpu.SemaphoreType.DMA((N_DEV - 1,)), pltpu.SemaphoreType.DMA((N_DEV - 1,)), pltpu.SemaphoreType.DMA],
    )(blocks)


def sum_blocks(blocks, name):
    _, r, c = blocks.shape
    ts = _row_tile(r, (N_DEV * blocks.dtype.itemsize + 4) * c)

    def body(x_ref, o_ref):
        acc = x_ref[0].astype(F32)
        for d in range(1, N_DEV):
            acc = acc + x_ref[d].astype(F32)
        o_ref[...] = acc

    return pl.pallas_call(
        body, name=name,
        out_shape=jax.ShapeDtypeStruct((r, c), F32),
        grid=(r // ts,),
        in_specs=[pl.BlockSpec((N_DEV, ts, c), lambda i: (0, i, 0))],
        out_specs=_row_spec(ts, c),
        compiler_params=_cparams(("parallel",), 2 * ts * c * (N_DEV * blocks.dtype.itemsize + 4)),
    )(blocks)


def all_reduce_small(v, name):
    r, c = v.shape

    def body(x_ref, out_ref, gathered, send_sems, recv_sems):
        me, peers, peer_ids = _me_and_peers()
        sends = [pltpu.make_async_remote_copy(src_ref=x_ref, dst_ref=gathered.at[me], send_sem=send_sems.at[i],
                                              recv_sem=recv_sems.at[i], device_id=p, device_id_type=MESH)
                 for i, p in enumerate(peers)]
        for cp in sends:
            cp.start()
        gathered[me] = x_ref[...]
        for i, (p, pid) in enumerate(zip(peers, peer_ids)):
            pltpu.make_async_remote_copy(src_ref=x_ref, dst_ref=gathered.at[pid], send_sem=send_sems.at[i],
                                         recv_sem=recv_sems.at[i], device_id=p, device_id_type=MESH).wait_recv()
        for cp in sends:
            cp.wait_send()
        acc = gathered[0]
        for d in range(1, N_DEV):
            acc = acc + gathered[d]
        out_ref[...] = acc

    return pl.pallas_call(
        body, name=name,
        out_shape=jax.ShapeDtypeStruct((r, c), F32),
        in_specs=[pl.BlockSpec(memory_space=pltpu.VMEM)],
        out_specs=pl.BlockSpec(memory_space=pltpu.VMEM),
        scratch_shapes=[pltpu.VMEM((N_DEV, r, c), F32), pltpu.SemaphoreType.DMA((N_DEV - 1,)), pltpu.SemaphoreType.DMA((N_DEV - 1,))],
    )(v)


_SHARD_AXIS = {"w_in": 1, "mla_w_qb": 1, "mla_w_kvb": 2, "w_mem_kv": 1, "w_merge_up": 2, "w_branch": 3, "w_out": 1}
_SMALL = ("mla_q_norm", "mla_kv_norm", "fox_forget_bias", "ln_gain", "ln_bias")

def _regroup_qb(w):
    lead = w.shape[:-1]
    t = w.reshape(lead + (N_HEADS, MLA_NOPE + MLA_ROPE))
    half = MLA_ROPE // 2
    parts = (t[..., :MLA_NOPE], t[..., MLA_NOPE:MLA_NOPE + half], t[..., MLA_NOPE + half:])
    return jnp.concatenate([p.reshape(lead + (-1,)) for p in parts], axis=-1)


def _ungroup_qb(g):
    lead = g.shape[:-1]
    half = MLA_ROPE // 2
    n_nope, n_half = N_HEADS * MLA_NOPE, N_HEADS * half
    parts = (g[..., :n_nope].reshape(lead + (N_HEADS, MLA_NOPE)), g[..., n_nope:n_nope + n_half].reshape(lead + (N_HEADS, half)),
             g[..., n_nope + n_half:].reshape(lead + (N_HEADS, half)))
    return jnp.concatenate(parts, axis=-1).reshape(lead + (-1,))


def _to_wire(name, w):
    if name == "w_in":
        return jnp.pad(w, ((0, 0), (0, 0), (0, IN_WIDTH_PAD - IN_WIDTH)))
    if name == "mla_w_qb":
        return _regroup_qb(w)
    return w


def _from_wire(name, g):
    if name == "w_in":
        return g[:, :, :IN_WIDTH]
    if name == "mla_w_qb":
        return _ungroup_qb(g)
    return g


def _join(name, blocks):
    a = _SHARD_AXIS[name]
    t = jnp.moveaxis(blocks, 0, a)
    return t.reshape(t.shape[:a] + (t.shape[a] * t.shape[a + 1],) + t.shape[a + 2:])


def _cut(name, full):
    a = _SHARD_AXIS[name]
    t = full.reshape(full.shape[:a] + (N_DEV, full.shape[a] // N_DEV) + full.shape[a + 1:])
    return jnp.moveaxis(t, a, 0)


def _heads(t):
    s = t.shape[0]
    return t.reshape(s, N_HEADS, -1).transpose(1, 0, 2)


def _unheads(t):
    return t.transpose(1, 0, 2).reshape(t.shape[1], -1)


def _layer(x, mem, rope, w, p, tag):
    s = x.shape[0]
    cos, sin, cos4, sin4 = rope
    c_q, c_kv, k_rope, sb_qkv, fox_qkv, fox_f, mem_q, gate_z, merge_r = in_proj(x, w["w_in"], IN_SPLITS, tag + "in_proj")

    (cq_n,) = rowwise(_rms_fn, [c_q], [p["mla_q_norm"]], [MLA_Q_RANK], tag + "q_rms")
    q_all = matmul(cq_n, w["mla_w_qb"], tag + "q_up")
    n_nope, n_half = N_HEADS * MLA_NOPE, N_HEADS * MLA_ROPE // 2
    q_nope, q_x1, q_x2 = q_all[:, :n_nope], q_all[:, n_nope:n_nope + n_half], q_all[:, n_nope + n_half:]
    (ckv_n,) = rowwise(_rms_fn, [c_kv], [p["mla_kv_norm"]], [MLA_KV_RANK], tag + "kv_rms")
    kv = matmul(ckv_n, w["mla_w_kvb"], tag + "kv_up").reshape(s, N_HEADS, MLA_NOPE + HEAD_DIM)
    k_nope, v_mla = kv[..., :MLA_NOPE], kv[..., MLA_NOPE:]
    half = MLA_ROPE // 2
    q_o1, q_o2 = rowwise(_rope_fn, [q_x1, q_x2, cos4, sin4], [], [n_half, n_half], tag + "q_rope")
    k_o1, k_o2 = rowwise(_rope_fn, [k_rope[:, :half], k_rope[:, half:], cos, sin], [], [half, half], tag + "k_rope")
    zeros = jnp.zeros((s, N_HEADS, MLA_QK_PAD - MLA_NOPE - MLA_ROPE), F32)
    q_mla = jnp.concatenate([q_nope.reshape(s, N_HEADS, MLA_NOPE), q_o1.reshape(s, N_HEADS, half),
                             q_o2.reshape(s, N_HEADS, half), zeros], axis=-1).transpose(1, 0, 2)
    k_mla = jnp.concatenate([k_nope, jnp.broadcast_to(k_o1[:, None, :], (s, N_HEADS, half)),
                             jnp.broadcast_to(k_o2[:, None, :], (s, N_HEADS, half)), zeros], axis=-1).transpose(1, 0, 2)
    y_mla = softmax_attention(q_mla, k_mla, v_mla.transpose(1, 0, 2), None,
                              (MLA_NOPE + MLA_ROPE) ** -0.5, True, tag + "mla_attn")

    width = N_HEADS * HEAD_DIM
    sq, sk, sv = (_heads(sb_qkv[:, i * width:(i + 1) * width]) for i in range(3))
    y_sb = stick_breaking_attention(sq, sk, sv, HEAD_DIM ** -0.5, tag + "stick_attn")

    fq, fk, fv = (_heads(fox_qkv[:, i * width:(i + 1) * width]) for i in range(3))
    (log_f,) = rowwise(_forget_fn, [fox_f], [p["fox_forget_bias"]], [N_HEADS], tag + "forget_gate")
    log_f8 = jnp.pad(log_f.T, ((0, SUBLANES - N_HEADS), (0, 0)))
    c = cumsum_lanes(log_f8, tag + "forget_cumsum")[:N_HEADS]
    y_fox = softmax_attention(fq, fk, fv, c, HEAD_DIM ** -0.5, True, tag + "fox_attn")

    mkv = matmul(mem, w["w_mem_kv"], tag + "mem_kv")
    y_mem = softmax_attention(_heads(mem_q), _heads(mkv[:, :width]), _heads(mkv[:, width:]), None,
                              HEAD_DIM ** -0.5, False, tag + "mem_attn")

    ys = [_unheads(t) for t in (y_mla, y_sb, y_fox, y_mem)]
    yb = rowwise(_silu_gate_fn, ys + [gate_z], [], [BRANCH_WIDTH] * N_BRANCHES, tag + "silu_gate")
    proj = [matmul(yb[n], w["w_branch"][n], tag + "branch%d" % n) for n in range(N_BRANCHES)]
    merged = merge_gate(merge_r, w["w_merge_up"], proj, tag + "merge")
    out = matmul(merged, w["w_out"], tag + "out_proj")
    (xn,) = rowwise(_deepnorm_fn, [x, out], [p["ln_gain"], p["ln_bias"]], [x.shape[1]], tag + "deepnorm")
    return xn


def _adamw(g, w, m, v, name):
    shape = g.shape
    two_d = (-1, shape[-1])
    g2, w2, m2, v2 = (t.reshape(two_d) for t in (g, w, m, v))
    delta, new_m, new_v = _rowwise_fwd_call(_adamw_fn, [g2, w2, m2, v2], [], [shape[-1]] * 3, name)
    return delta.reshape(shape), new_m.reshape(shape), new_v.reshape(shape)


def kernel(x, mem, positions, w_in, mla_q_norm, mla_w_qb, mla_kv_norm, mla_w_kvb, fox_forget_bias, w_mem_kv, w_merge_up, w_branch, w_out, ln_gain, ln_bias, loss_target, m_w_in, m_mla_q_norm, m_mla_w_qb, m_mla_kv_norm, m_mla_w_kvb, m_fox_forget_bias, m_w_mem_kv, m_w_merge_up, m_w_branch, m_w_out, m_ln_gain, m_ln_bias, v_w_in, v_mla_q_norm, v_mla_w_qb, v_mla_kv_norm, v_mla_w_kvb, v_fox_forget_bias, v_w_mem_kv, v_w_merge_up, v_w_branch, v_w_out, v_ln_gain, v_ln_bias):
    weights = dict(w_in=w_in, mla_q_norm=mla_q_norm, mla_w_qb=mla_w_qb, mla_kv_norm=mla_kv_norm, mla_w_kvb=mla_w_kvb,
                   fox_forget_bias=fox_forget_bias, w_mem_kv=w_mem_kv, w_merge_up=w_merge_up, w_branch=w_branch,
                   w_out=w_out, ln_gain=ln_gain, ln_bias=ln_bias)
    m_in = dict(w_in=m_w_in, mla_q_norm=m_mla_q_norm, mla_w_qb=m_mla_w_qb, mla_kv_norm=m_mla_kv_norm, mla_w_kvb=m_mla_w_kvb,
                fox_forget_bias=m_fox_forget_bias, w_mem_kv=m_w_mem_kv, w_merge_up=m_w_merge_up, w_branch=m_w_branch,
                w_out=m_w_out, ln_gain=m_ln_gain, ln_bias=m_ln_bias)
    v_in = dict(w_in=v_w_in, mla_q_norm=v_mla_q_norm, mla_w_qb=v_mla_w_qb, mla_kv_norm=v_mla_kv_norm, mla_w_kvb=v_mla_w_kvb,
                fox_forget_bias=v_fox_forget_bias, w_mem_kv=v_w_mem_kv, w_merge_up=v_w_merge_up, w_branch=v_w_branch,
                w_out=v_w_out, ln_gain=v_ln_gain, ln_bias=v_ln_bias)
    order = list(weights)
    big = list(_SHARD_AXIS)
    n_layers = w_in.shape[0]
    x0, mem0 = x[0], mem[0]
    s = x0.shape[0]

    wire = [_to_wire(n, weights[n]).astype(WIRE_DTYPE) for n in big]
    sizes = [t.size for t in wire]
    flat = jnp.concatenate([t.reshape(-1) for t in wire]).reshape(-1, LANES)
    gathered = all_gather_hbm(flat, "gather_weights").reshape(N_DEV, -1)
    offs = np.cumsum([0] + sizes)
    full = {n: _join(n, gathered[:, offs[i]:offs[i + 1]].reshape((N_DEV,) + wire[i].shape)) for i, n in enumerate(big)}
    per_layer_w = [{n: full[n][l] for n in big} for l in range(n_layers)]
    per_layer_p = [{n: weights[n][l][None, :] for n in _SMALL} for l in range(n_layers)]

    inv_freq = ROPE_THETA ** (-jnp.arange(0, MLA_ROPE, 2, dtype=F32) / MLA_ROPE)
    ang = positions[0].astype(F32)[:, None] * inv_freq
    cos, sin = jnp.cos(ang), jnp.sin(ang)
    rope = (cos, sin, jnp.tile(cos, (1, N_HEADS)), jnp.tile(sin, (1, N_HEADS)))

    def trunk(x, ws, ps):
        for l in range(n_layers):
            x = _layer(x, mem0, rope, ws[l], ps[l], "l%d_" % l)
        return x

    y, pullback = jax.vjp(trunk, x0, per_layer_w, per_layer_p)
    loss_part, dy = _loss_call(y, loss_target[0])
    grad_x, grad_ws, grad_ps = pullback(dy)

    cut = [_cut(n, jnp.stack([grad_ws[l][n] for l in range(n_layers)])) for n in big]
    blocks = jnp.concatenate([t.reshape(N_DEV, -1) for t in cut], axis=1).reshape(N_DEV, -1, LANES)
    received = all_to_all_hbm(blocks, "scatter_grads")
    summed = sum_blocks(received, "sum_grads").reshape(-1)
    grads = {n: _from_wire(n, summed[offs[i]:offs[i + 1]].reshape(wire[i].shape)) for i, n in enumerate(big)}

    small = [jnp.stack([grad_ps[l][n][0] for l in range(n_layers)]) for n in _SMALL]
    small_sizes = [t.size for t in small]
    packed = jnp.concatenate([t.reshape(-1) for t in small] + [loss_part.reshape(1)])
    rows = -(-packed.size // (SUBLANES * LANES)) * SUBLANES
    packed = jnp.pad(packed, (0, rows * LANES - packed.size)).reshape(rows, LANES)
    reduced = all_reduce_small(packed, "reduce_small").reshape(-1)
    small_offs = np.cumsum([0] + small_sizes)
    for i, n in enumerate(_SMALL):
        grads[n] = reduced[small_offs[i]:small_offs[i + 1]].reshape(small[i].shape)
    loss = reduced[small_offs[-1]]

    delta, new_m, new_v = {}, {}, {}
    for n in order:
        delta[n], new_m[n], new_v[n] = _adamw(grads[n], weights[n], m_in[n], v_in[n], "adamw_" + n)
    return (loss, grad_x[None], *[grads[n] for n in order], *[delta[n] for n in order],
            *[new_m[n] for n in order], *[new_v[n] for n in order])
```

```python
import functools

import numpy as np
import jax
import jax.numpy as jnp
from jax import lax
from jax.experimental import pallas as pl
from jax.experimental.pallas import tpu as pltpu

F32 = jnp.float32
MXU_DTYPE = jnp.bfloat16
WIRE_DTYPE = jnp.bfloat16

N_LAYERS = 4
N_BRANCHES = 4
BRANCH_WIDTH = 256
N_HEADS = 4
HEAD_DIM = 64
MLA_NOPE = 64
MLA_ROPE = 32
MLA_Q_RANK = 384
MLA_KV_RANK = 256
ROPE_THETA = 10000.0
MERGE_RANK = 128
RMS_EPS = 1e-6
LN_EPS = 1e-5
DEEPNORM_ALPHA = (2 * N_LAYERS) ** 0.25
IN_SPLITS = (MLA_Q_RANK, MLA_KV_RANK, MLA_ROPE, 3 * N_HEADS * HEAD_DIM, 3 * N_HEADS * HEAD_DIM,
             N_HEADS, N_HEADS * HEAD_DIM, N_BRANCHES * BRANCH_WIDTH, MERGE_RANK)
IN_WIDTH = sum(IN_SPLITS)
ADAM_LR, ADAM_B1, ADAM_B2, ADAM_EPS, ADAM_WD, ADAM_STEP = 0.001, 0.9, 0.999, 1e-08, 0.01, 10

LANES = 128
SUBLANES = 8
VMEM_BYTES_V7X = 64 * 1024 * 1024
VMEM_LIMIT_CAP = VMEM_BYTES_V7X - 8 * 1024 * 1024
MM_VMEM_BUDGET = 44 * 1024 * 1024
ROW_TILE_BYTES = 8 * 1024 * 1024
ATTN_TILE = 512
SUFFIX_BLOCK = 256
N_DEV = 8
IN_WIDTH_PAD = -(-IN_WIDTH // LANES) * LANES
MLA_QK_PAD = 128


def _cparams(semantics, vmem_estimate):
    limit = int(min(max(2 * vmem_estimate, 32 * 1024 * 1024), VMEM_LIMIT_CAP))
    return pltpu.CompilerParams(dimension_semantics=semantics, vmem_limit_bytes=limit)


def _dot(a, b, ca=1, cb=0):
    return lax.dot_general(a, b, (((ca,), (cb,)), ((), ())), preferred_element_type=F32)


def _div_tile(n, cap, align):
    t = (cap // align) * align
    while t >= align:
        if n % t == 0:
            return t
        t -= align
    return n


def _mm_call(a, b, ta, tb, out_dtype, name):
    k, m = a.shape if ta else a.shape[::-1]
    n, kb = b.shape if tb else b.shape[::-1]
    assert k == kb, (a.shape, b.shape, ta, tb)
    ab, bb, ob = a.dtype.itemsize, b.dtype.itemsize, jnp.dtype(out_dtype).itemsize
    tk = k if k <= 4096 else 512
    tn = n if n <= 4096 else _div_tile(n, 2048, LANES)
    cands = [t for t in (1024, 512, 256, 128) if t <= m and m % t == 0] or [m]
    for tm in cands:
        est = 2 * (tm * tk * ab + tk * tn * bb + tm * tn * ob) + (tm * tn * 4 if k > tk else 0)
        if est <= MM_VMEM_BUDGET:
            break
    nk = k // tk

    def body(a_ref, b_ref, o_ref, *scratch):
        p = _dot(a_ref[...].astype(MXU_DTYPE), b_ref[...].astype(MXU_DTYPE), 0 if ta else 1, 1 if tb else 0)
        if nk == 1:
            o_ref[...] = p.astype(o_ref.dtype)
        else:
            acc = scratch[0]
            kk = pl.program_id(2)

            @pl.when(kk == 0)
            def _():
                acc[...] = p

            @pl.when(kk > 0)
            def _():
                acc[...] += p

            @pl.when(kk == nk - 1)
            def _():
                o_ref[...] = acc[...].astype(o_ref.dtype)

    a_spec = pl.BlockSpec((tk, tm), lambda i, j, l: (l, i)) if ta else pl.BlockSpec((tm, tk), lambda i, j, l: (i, l))
    b_spec = pl.BlockSpec((tn, tk), lambda i, j, l: (j, l)) if tb else pl.BlockSpec((tk, tn), lambda i, j, l: (l, j))
    return pl.pallas_call(
        body, name=name,
        out_shape=jax.ShapeDtypeStruct((m, n), out_dtype),
        grid=(m // tm, n // tn, nk),
        in_specs=[a_spec, b_spec],
        out_specs=pl.BlockSpec((tm, tn), lambda i, j, l: (i, j)),
        scratch_shapes=[pltpu.VMEM((tm, tn), F32)] if nk > 1 else [],
        compiler_params=_cparams(("parallel", "parallel", "arbitrary"), est),
    )(a, b)


def matmul(a, b, name):
    @jax.custom_vjp
    def op(a, b):
        return _mm_call(a, b, False, False, F32, name)

    def fwd(a, b):
        return op(a, b), (a, b)

    def bwd(res, g):
        a, b = res
        da = _mm_call(g, b, False, True, a.dtype, name + "_da")
        db = _mm_call(a, g, True, False, b.dtype, name + "_db")
        return da, db

    op.defvjp(fwd, bwd)
    return op(a, b)


def _row_tile(n_rows, bytes_per_row):
    align = 2 * SUBLANES
    cap = max(align, ROW_TILE_BYTES // bytes_per_row)
    best = 0
    for t in range(align, min(n_rows, cap, 2048) + 1, align):
        if n_rows % t == 0:
            best = t
    return best or n_rows


def _row_spec(ts, cols):
    return pl.BlockSpec((ts, cols), lambda i: (i, 0))


def _full_spec(shape):
    return pl.BlockSpec(shape, lambda i: (0,) * len(shape))


def _rowwise_fwd_call(f, rows, params, out_cols, name):
    n = rows[0].shape[0]
    per_row = 4 * (sum(r.shape[1] for r in rows) + sum(out_cols))
    ts = _row_tile(n, per_row)
    n_in, n_p = len(rows), len(params)

    def body(*refs):
        ins = [r[...] for r in refs[:n_in + n_p]]
        outs = f(*ins)
        for o_ref, o in zip(refs[n_in + n_p:], outs):
            o_ref[...] = o.astype(o_ref.dtype)

    return pl.pallas_call(
        body, name=name,
        out_shape=[jax.ShapeDtypeStruct((n, c), F32) for c in out_cols],
        grid=(n // ts,),
        in_specs=[_row_spec(ts, r.shape[1]) for r in rows] + [_full_spec(p.shape) for p in params],
        out_specs=[_row_spec(ts, c) for c in out_cols],
        compiler_params=_cparams(("parallel",), 2 * ts * per_row),
    )(*rows, *params)


def _rowwise_bwd_call(f, rows, params, cts, name):
    n = rows[0].shape[0]
    per_row = 4 * (2 * sum(r.shape[1] for r in rows) + 2 * sum(c.shape[1] for c in cts))
    ts = _row_tile(n, per_row)
    n_in, n_p, n_ct = len(rows), len(params), len(cts)

    def body(*refs):
        ins = [r[...] for r in refs[:n_in + n_p]]
        ct = tuple(r[...] for r in refs[n_in + n_p:n_in + n_p + n_ct])
        grads = jax.vjp(lambda *a: tuple(f(*a)), *ins)[1](ct)
        out_refs = refs[n_in + n_p + n_ct:]
        for o_ref, g in zip(out_refs[:n_in], grads[:n_in]):
            o_ref[...] = g
        first = pl.program_id(0) == 0
        for o_ref, g in zip(out_refs[n_in:], grads[n_in:]):
            @pl.when(first)
            def _(o_ref=o_ref, g=g):
                o_ref[...] = g

            @pl.when(jnp.logical_not(first))
            def _(o_ref=o_ref, g=g):
                o_ref[...] += g

    outs = pl.pallas_call(
        body, name=name,
        out_shape=[jax.ShapeDtypeStruct(r.shape, F32) for r in rows] + [jax.ShapeDtypeStruct(p.shape, F32) for p in params],
        grid=(n // ts,),
        in_specs=([_row_spec(ts, r.shape[1]) for r in rows] + [_full_spec(p.shape) for p in params]
                  + [_row_spec(ts, c.shape[1]) for c in cts]),
        out_specs=[_row_spec(ts, r.shape[1]) for r in rows] + [_full_spec(p.shape) for p in params],
        compiler_params=_cparams(("arbitrary",), 2 * ts * per_row),
    )(*rows, *params, *cts)
    return outs[:n_in], outs[n_in:]


def rowwise(f, rows, params, out_cols, name):
    @jax.custom_vjp
    def op(rows, params):
        return tuple(_rowwise_fwd_call(f, rows, params, out_cols, name))

    def fwd(rows, params):
        return op(rows, params), (rows, params)

    def bwd(res, cts):
        rows, params = res
        d_rows, d_params = _rowwise_bwd_call(f, rows, params, cts, name + "_bwd")
        return tuple(d_rows), tuple(d_params)

    op.defvjp(fwd, bwd)
    return op(tuple(rows), tuple(params))


def _rms_fn(x, g):
    return (x * lax.rsqrt(jnp.mean(x * x, axis=-1, keepdims=True) + RMS_EPS) * g,)


def _rope_fn(x1, x2, cos, sin):
    return x1 * cos - x2 * sin, x1 * sin + x2 * cos


def _log_sigmoid(x):
    return jnp.minimum(x, 0.0) - jnp.log1p(jnp.exp(-jnp.abs(x)))


def _forget_fn(f, bias):
    return (_log_sigmoid(f + bias),)


def _silu_gate_fn(y0, y1, y2, y3, gate_z):
    return tuple(y * (z * jax.nn.sigmoid(z))
                 for y, z in zip((y0, y1, y2, y3), (gate_z[:, n * BRANCH_WIDTH:(n + 1) * BRANCH_WIDTH] for n in range(N_BRANCHES))))


def _deepnorm_fn(x, out, g, b):
    h = DEEPNORM_ALPHA * x + out
    mu = jnp.mean(h, axis=-1, keepdims=True)
    var = jnp.mean(jnp.square(h - mu), axis=-1, keepdims=True)
    return ((h - mu) * lax.rsqrt(var + LN_EPS) * g + b,)


def _adamw_fn(g, w, m, v):
    m = ADAM_B1 * m + (1.0 - ADAM_B1) * g
    v = ADAM_B2 * v + (1.0 - ADAM_B2) * jnp.square(g)
    m_hat = m / (1.0 - ADAM_B1 ** ADAM_STEP)
    v_hat = v / (1.0 - ADAM_B2 ** ADAM_STEP)
    delta = -ADAM_LR * (m_hat / (jnp.sqrt(v_hat) + ADAM_EPS) + ADAM_WD * w)
    return delta, m, v


def in_proj(x, w, sizes, name):
    offs = np.cumsum((0,) + tuple(sizes))
    pad = w.shape[1] - int(offs[-1])

    @jax.custom_vjp
    def op(x, w):
        h = _mm_call(x, w, False, False, F32, name)
        return tuple(h[:, offs[i]:offs[i + 1]] for i in range(len(sizes)))

    def fwd(x, w):
        return op(x, w), (x, w)

    def bwd(res, cts):
        x, w = res
        parts = [c.astype(MXU_DTYPE) for c in cts] + ([jnp.zeros((x.shape[0], pad), MXU_DTYPE)] if pad else [])
        dh = jnp.concatenate(parts, axis=1)
        return _mm_call(dh, w, False, True, x.dtype, name + "_da"), _mm_call(x, dh, True, False, w.dtype, name + "_db")

    op.defvjp(fwd, bwd)
    return op(x, w)


MERGE_ROWS = 256


def _merge_fwd_call(r, w_up, ps, name):
    n, rank = r.shape
    d = ps[0].shape[1]
    nb = len(ps)
    ts = min(MERGE_ROWS, n)

    def body(r_ref, w_ref, *refs):
        rb = r_ref[...].astype(MXU_DTYPE)
        acc = jnp.zeros((ts, d), F32)
        for i in range(nb):
            gate = jax.nn.sigmoid(_dot(rb, w_ref[:, i * d:(i + 1) * d]))
            acc = acc + gate * refs[i][...]
        refs[nb][...] = acc

    return pl.pallas_call(
        body, name=name,
        out_shape=jax.ShapeDtypeStruct((n, d), F32),
        grid=(n // ts,),
        in_specs=[_row_spec(ts, rank), _full_spec(w_up.shape)] + [_row_spec(ts, d)] * nb,
        out_specs=_row_spec(ts, d),
        compiler_params=_cparams(("parallel",), 2 * 4 * ts * d * (nb + 2)),
    )(r, w_up, *ps)


def _merge_bwd_call(r, w_up, ps, dm, name):
    n, rank = r.shape
    d = ps[0].shape[1]
    nb = len(ps)
    ts = min(MERGE_ROWS, n)
    steps = n // ts

    def body(r_ref, w_ref, *refs):
        p_refs, dm_ref = refs[:nb], refs[nb]
        dr_ref, dw_ref, dp_refs, dw_acc = refs[nb + 1], refs[nb + 2], refs[nb + 3:2 * nb + 3], refs[2 * nb + 3]
        step = pl.program_id(0)

        @pl.when(step == 0)
        def _():
            dw_acc[...] = jnp.zeros_like(dw_acc)

        rb = r_ref[...].astype(MXU_DTYPE)
        dmv = dm_ref[...]
        dr = jnp.zeros((ts, rank), F32)
        for i in range(nb):
            w_i = w_ref[:, i * d:(i + 1) * d]
            gate = jax.nn.sigmoid(_dot(rb, w_i))
            dp_refs[i][...] = dmv * gate
            d_logit = (dmv * p_refs[i][...] * gate * (1.0 - gate)).astype(MXU_DTYPE)
            dr = dr + _dot(d_logit, w_i, 1, 1)
            dw_acc[:, i * d:(i + 1) * d] += _dot(rb, d_logit, 0, 0)
        dr_ref[...] = dr

        @pl.when(step == steps - 1)
        def _():
            dw_ref[...] = dw_acc[...].astype(dw_ref.dtype)

    outs = pl.pallas_call(
        body, name=name,
        out_shape=[jax.ShapeDtypeStruct((n, rank), F32), jax.ShapeDtypeStruct(w_up.shape, w_up.dtype)]
        + [jax.ShapeDtypeStruct((n, d), F32)] * nb,
        grid=(steps,),
        in_specs=[_row_spec(ts, rank), _full_spec(w_up.shape)] + [_row_spec(ts, d)] * (nb + 1),
        out_specs=[_row_spec(ts, rank), _full_spec(w_up.shape)] + [_row_spec(ts, d)] * nb,
        scratch_shapes=[pltpu.VMEM(w_up.shape, F32)],
        compiler_params=_cparams(("arbitrary",), 2 * 4 * ts * d * (2 * nb + 2) + 8 * w_up.size),
    )(r, w_up, *ps, dm)
    return outs[0], outs[1], outs[2:]


def merge_gate(r, w_up, ps, name):
    @jax.custom_vjp
    def op(r, w_up, ps):
        return _merge_fwd_call(r, w_up, ps, name)

    def fwd(r, w_up, ps):
        return op(r, w_up, ps), (r, w_up, ps)

    def bwd(res, dm):
        r, w_up, ps = res
        dr, dw, dps = _merge_bwd_call(r, w_up, ps, dm, name + "_bwd")
        return dr, dw, tuple(dps)

    op.defvjp(fwd, bwd)
    return op(r, w_up, tuple(ps))


LOG2E = 1.4426950408889634


def _to_lanes(t):
    return t.transpose(0, 2, 1)


def _key_tiles(t, tk):
    h, s, d = t.shape
    return t.reshape(h, s // tk, tk, d).transpose(0, 1, 3, 2)


def _attn_vmem(sk, dq, dv, tq, tk, backward):
    resident = 2 * 2 * sk * (dq + dv) * (2 if backward else 1)
    grads = 2 * 4 * sk * (dq + dv) if backward else 0
    return resident + grads + 24 * tq * tk * 4


def _tile_ids(q0, k0, tq, tk):
    key = k0 + lax.broadcasted_iota(jnp.int32, (tk, tq), 0)
    qry = q0 + lax.broadcasted_iota(jnp.int32, (tk, tq), 1)
    return key, qry


def _lane_spec(d, tq):
    return pl.BlockSpec((1, d, tq), lambda hh, i: (hh, 0, i))


def _row_major_spec(tq, d):
    return pl.BlockSpec((1, tq, d), lambda hh, i: (hh, i, 0))


def _resident_spec(shape):
    return pl.BlockSpec((1,) + shape, lambda hh, i: (hh,) + (0,) * len(shape))


def _softmax_fwd_call(qt, k, vt, causal, name):
    h, dq, s = qt.shape
    sk, (n_kb, dv, tk) = k.shape[1], vt.shape[1:]
    tq = min(ATTN_TILE, s)
    assert not causal or (tq == tk and s == sk)

    def body(q_ref, k_ref, v_ref, o_ref, lse_ref):
        qi = pl.program_id(1)
        q_t = q_ref[0]

        def tile(kb, carry, masked):
            m, l, acc = carry
            off = pl.multiple_of(kb * tk, tk)
            st = _dot(k_ref[0, pl.ds(off, tk), :], q_t)
            if masked:
                key, qry = _tile_ids(qi * tq, kb * tk, tq, tk)
                st = jnp.where(key <= qry, st, -jnp.inf)
            m_new = jnp.maximum(m, jnp.max(st, axis=0, keepdims=True))
            alpha = jnp.exp2(m - m_new)
            p = jnp.exp2(st - m_new)
            l = alpha * l + jnp.sum(p, axis=0, keepdims=True)
            acc = alpha * acc + _dot(v_ref[0, kb], p.astype(MXU_DTYPE))
            return m_new, l, acc

        def pair(i, carry):
            m, l, acc = carry
            off = pl.multiple_of(2 * i * tk, 2 * tk)
            st = _dot(k_ref[0, pl.ds(off, 2 * tk), :], q_t)
            m_new = jnp.maximum(m, jnp.max(st, axis=0, keepdims=True))
            alpha = jnp.exp2(m - m_new)
            p = jnp.exp2(st - m_new)
            l = alpha * l + jnp.sum(p, axis=0, keepdims=True)
            p = p.astype(MXU_DTYPE)
            acc = alpha * acc + _dot(v_ref[0, 2 * i], p[:tk]) + _dot(v_ref[0, 2 * i + 1], p[tk:])
            return m_new, l, acc

        carry = (jnp.full((1, tq), -jnp.inf, F32), jnp.zeros((1, tq), F32), jnp.zeros((dv, tq), F32))
        n_full = qi if causal else n_kb
        if n_kb >= 2:
            carry = lax.fori_loop(0, n_full // 2, pair, carry)
        if causal:
            carry = lax.cond(n_full % 2 == 1, lambda c: tile(n_full - 1, c, False), lambda c: c, carry)
            carry = tile(qi, carry, True)
        elif n_kb % 2:
            carry = tile(n_kb - 1, carry, False)
        m, l, acc = carry
        o_ref[0] = acc / l
        lse_ref[0] = m + jnp.log2(l)

    return pl.pallas_call(
        body, name=name,
        out_shape=[jax.ShapeDtypeStruct((h, dv, s), F32), jax.ShapeDtypeStruct((h, 1, s), F32)],
        grid=(h, s // tq),
        in_specs=[_lane_spec(dq, tq), _resident_spec((sk, dq)), _resident_spec((n_kb, dv, tk))],
        out_specs=[_lane_spec(dv, tq), _lane_spec(1, tq)],
        compiler_params=_cparams(("parallel", "arbitrary"), _attn_vmem(sk, dq, dv, tq, tk, False)),
    )(qt, k, vt)


def _softmax_bwd_call(qt, q, k, kt, v, ot, lse, dot, do, causal, name):
    h, dq, s = qt.shape
    sk, dv = k.shape[1], v.shape[2]
    n_kb, tk = kt.shape[1], kt.shape[3]
    tq = min(ATTN_TILE, s)

    def body(qt_ref, q_ref, k_ref, kt_ref, v_ref, ot_ref, lse_ref, dot_ref, do_ref, gq_ref, gk_ref, gv_ref):
        qi = pl.program_id(1)

        @pl.when(qi == 0)
        def _():
            gk_ref[...] = jnp.zeros_like(gk_ref)
            gv_ref[...] = jnp.zeros_like(gv_ref)

        q_t, q_n, do_n = qt_ref[0], q_ref[0], do_ref[0]
        do_t = dot_ref[0]
        do_tb = do_t.astype(MXU_DTYPE)
        delta = jnp.sum(do_t * ot_ref[0], axis=0, keepdims=True)
        lse_q = lse_ref[0]

        def tile(kb, gq_acc, masked):
            off = pl.multiple_of(kb * tk, tk)
            st = _dot(k_ref[0, pl.ds(off, tk), :], q_t)
            if masked:
                key, qry = _tile_ids(qi * tq, kb * tk, tq, tk)
                st = jnp.where(key <= qry, st, -jnp.inf)
            p = jnp.exp2(st - lse_q)
            dp = _dot(v_ref[0, pl.ds(off, tk), :], do_tb)
            ds = (p * (dp - delta)).astype(MXU_DTYPE)
            gk_ref[0, pl.ds(off, tk), :] += _dot(ds, q_n)
            gv_ref[0, pl.ds(off, tk), :] += _dot(p.astype(MXU_DTYPE), do_n)
            return gq_acc + _dot(kt_ref[0, kb], ds)

        def pair(i, gq_acc):
            off = pl.multiple_of(2 * i * tk, 2 * tk)
            p = jnp.exp2(_dot(k_ref[0, pl.ds(off, 2 * tk), :], q_t) - lse_q)
            dp = _dot(v_ref[0, pl.ds(off, 2 * tk), :], do_tb)
            ds = (p * (dp - delta)).astype(MXU_DTYPE)
            gk_ref[0, pl.ds(off, 2 * tk), :] += _dot(ds, q_n)
            gv_ref[0, pl.ds(off, 2 * tk), :] += _dot(p.astype(MXU_DTYPE), do_n)
            return gq_acc + _dot(kt_ref[0, 2 * i], ds[:tk]) + _dot(kt_ref[0, 2 * i + 1], ds[tk:])

        n_full = qi if causal else n_kb
        gq = jnp.zeros((dq, tq), F32)
        if n_kb >= 2:
            gq = lax.fori_loop(0, n_full // 2, pair, gq)
        if causal:
            gq = lax.cond(n_full % 2 == 1, lambda c: tile(n_full - 1, c, False), lambda c: c, gq)
            gq = tile(qi, gq, True)
        elif n_kb % 2:
            gq = tile(n_kb - 1, gq, False)
        gq_ref[0] = gq

    return pl.pallas_call(
        body, name=name,
        out_shape=[jax.ShapeDtypeStruct((h, dq, s), F32), jax.ShapeDtypeStruct((h, sk, dq), F32), jax.ShapeDtypeStruct((h, sk, dv), F32)],
        grid=(h, s // tq),
        in_specs=[_lane_spec(dq, tq), _row_major_spec(tq, dq), _resident_spec((sk, dq)), _resident_spec((n_kb, dq, tk)),
                  _resident_spec((sk, dv)), _lane_spec(dv, tq), _lane_spec(1, tq), _lane_spec(dv, tq), _row_major_spec(tq, dv)],
        out_specs=[_lane_spec(dq, tq), _resident_spec((sk, dq)), _resident_spec((sk, dv))],
        compiler_params=_cparams(("parallel", "arbitrary"), _attn_vmem(sk, dq, dv, tq, tk, True)),
    )(qt, q, k, kt, v, ot, lse, dot, do)


def _split_terms(x, n):
    info = jnp.finfo(MXU_DTYPE)
    terms = []
    for _ in range(n):
        t = lax.reduce_precision(x, info.nexp, info.nmant)
        terms.append(t)
        x = x - t
    return terms


def softmax_attention(q, k, v, c, scale, causal, name):
    decay = c is not None
    d = q.shape[2]
    tk = min(ATTN_TILE, k.shape[1])

    def widen(q, k, c):
        qs = q * (scale * LOG2E)
        if decay:
            terms = jnp.stack(_split_terms(c * LOG2E, 3), axis=-1)
            ones = jnp.ones_like(terms)
            pad = jnp.zeros(q.shape[:2] + (LANES - d - 6,), F32)
            qs = jnp.concatenate([qs, terms, ones, pad], axis=-1)
            k = jnp.concatenate([k, ones, -terms, pad], axis=-1)
        return qs.astype(MXU_DTYPE), k.astype(MXU_DTYPE)

    def run(q, k, v, c):
        qa, ka = widen(q, k, c)
        vb = v.astype(MXU_DTYPE)
        qt = _to_lanes(qa)
        ot, lse = _softmax_fwd_call(qt, ka, _key_tiles(vb, tk), causal, name)
        return _to_lanes(ot), (qt, qa, ka, vb, ot, lse)

    @jax.custom_vjp
    def op(q, k, v, c):
        return run(q, k, v, c)[0]

    def bwd(res, do):
        qt, qa, ka, vb, ot, lse = res
        gqt, gk, gv = _softmax_bwd_call(qt, qa, ka, _key_tiles(ka, tk), vb, ot, lse, _to_lanes(do),
                                        do.astype(MXU_DTYPE), causal, name + "_bwd")
        gq = _to_lanes(gqt)
        dc = gq[..., d] - gk[..., d + 3] if decay else None
        return gq[..., :d] * scale, gk[..., :d] * (1.0 / LOG2E), gv, dc

    op.defvjp(run, bwd)
    return op(q, k, v, c)


def _running_sum(ones, x, suffix):
    xb = x.astype(MXU_DTYPE)
    b = ones.shape[0]
    n = x.shape[0] // b
    out, carry = [None] * n, None
    for i in (reversed(range(n)) if suffix else range(n)):
        blk = _dot(ones, xb[i * b:(i + 1) * b])
        blk = blk if carry is None else blk + carry
        carry = blk[0:1, :] if suffix else blk[b - 1:b, :]
        out[i] = blk
    return out[0] if n == 1 else jnp.concatenate(out, axis=0)


def _ones_after(tk):
    b = min(tk, SUFFIX_BLOCK)
    return (lax.broadcasted_iota(jnp.int32, (b, b), 1) >= lax.broadcasted_iota(jnp.int32, (b, b), 0)).astype(MXU_DTYPE)


def _ones_before(tk):
    b = min(tk, SUFFIX_BLOCK)
    return (lax.broadcasted_iota(jnp.int32, (b, b), 1) <= lax.broadcasted_iota(jnp.int32, (b, b), 0)).astype(MXU_DTYPE)


LOGIT_CLAMP = 120.0


def _log_keep(k_tile, q_t, strict):
    z = jnp.minimum(_dot(k_tile, q_t), LOGIT_CLAMP)
    log_keep = -(jnp.log(1.0 + jnp.exp2(z)) * LOG2E)
    return z, (log_keep if strict is None else jnp.where(strict, log_keep, 0.0))


def _stick_weights(z, log_keep, later, ones_after, strict):
    local = _running_sum(ones_after, log_keep, True)
    a = jnp.exp2(z + local + later)
    return (a if strict is None else jnp.where(strict, a, 0.0)), local


def _strict_mask(qi, kb, t):
    key, qry = _tile_ids(qi * t, kb * t, t, t)
    return key < qry


def _stick_fwd_call(qt, k, vt, name):
    h, d, s = qt.shape
    t = vt.shape[3]
    n_t = s // t
    assert t == min(ATTN_TILE, s)

    def body(q_ref, k_ref, v_ref, o_ref, later_ref):
        qi = pl.program_id(1)
        q_t = q_ref[0]
        ones_after = _ones_after(t)
        later_ref[...] = jnp.zeros_like(later_ref)

        def tile(kb, carry, strict):
            later, acc = carry
            off = pl.multiple_of(kb * t, t)
            z, log_keep = _log_keep(k_ref[0, pl.ds(off, t), :], q_t, strict)
            a, local = _stick_weights(z, log_keep, later, ones_after, strict)
            later_ref[0, kb] = later
            return later + local[0:1, :], acc + _dot(v_ref[0, kb], a.astype(MXU_DTYPE))

        def pair(i, carry):
            later, acc = carry
            right = qi - 1 - 2 * i
            off = pl.multiple_of((right - 1) * t, t)
            z, log_keep = _log_keep(k_ref[0, pl.ds(off, 2 * t), :], q_t, None)
            a, local = _stick_weights(z, log_keep, later, ones_after, None)
            later_ref[0, right] = later
            later_ref[0, right - 1] = later + local[t:t + 1, :]
            a = a.astype(MXU_DTYPE)
            return later + local[0:1, :], acc + _dot(v_ref[0, right - 1], a[:t]) + _dot(v_ref[0, right], a[t:])

        carry = tile(qi, (jnp.zeros((1, t), F32), jnp.zeros((d, t), F32)), _strict_mask(qi, qi, t))
        if n_t >= 2:
            carry = lax.fori_loop(0, qi // 2, pair, carry)
        _, acc = lax.cond(qi % 2 == 1, lambda c: tile(0, c, None), lambda c: c, carry)
        o_ref[0] = acc

    later_spec = pl.BlockSpec((1, n_t, 1, t), lambda hh, i: (hh, 0, 0, i))
    return pl.pallas_call(
        body, name=name,
        out_shape=[jax.ShapeDtypeStruct((h, d, s), F32), jax.ShapeDtypeStruct((h, n_t, 1, s), F32)],
        grid=(h, n_t),
        in_specs=[_lane_spec(d, t), _resident_spec((s, d)), _resident_spec((n_t, d, t))],
        out_specs=[_lane_spec(d, t), later_spec],
        compiler_params=_cparams(("parallel", "arbitrary"), _attn_vmem(s, d, d, t, t, False)),
    )(qt, k, vt)


def _stick_bwd_call(qt, q, k, kt, v, later, dot, do, name):
    h, d, s = qt.shape
    t = kt.shape[3]
    n_t = s // t

    def body(qt_ref, q_ref, k_ref, kt_ref, v_ref, later_ref, dot_ref, do_ref, gq_ref, gk_ref, gv_ref):
        qi = pl.program_id(1)

        @pl.when(qi == 0)
        def _():
            gk_ref[...] = jnp.zeros_like(gk_ref)
            gv_ref[...] = jnp.zeros_like(gv_ref)

        q_t, q_n, do_n = qt_ref[0], q_ref[0], do_ref[0]
        do_tb = dot_ref[0].astype(MXU_DTYPE)
        ones_after, ones_before = _ones_after(t), _ones_before(t)
        diagonal = _strict_mask(qi, qi, t)

        def sweep_right(kb, carry, strict):
            before, gq_acc = carry
            off = pl.multiple_of(kb * t, t)
            z, log_keep = _log_keep(k_ref[0, pl.ds(off, t), :], q_t, strict)
            a, _ = _stick_weights(z, log_keep, later_ref[0, kb], ones_after, strict)
            g = a * _dot(v_ref[0, pl.ds(off, t), :], do_tb)
            g_prefix = _running_sum(ones_before, g, False) + before
            sig = 1.0 - jnp.exp2(log_keep)
            dz = g - sig * g_prefix
            dz = (dz if strict is None else jnp.where(strict, dz, 0.0)).astype(MXU_DTYPE)
            gk_ref[0, pl.ds(off, t), :] += _dot(dz, q_n)
            gv_ref[0, pl.ds(off, t), :] += _dot(a.astype(MXU_DTYPE), do_n)
            return g_prefix[t - 1:t, :], gq_acc + _dot(kt_ref[0, kb], dz)

        carry = lax.fori_loop(0, qi, lambda kb, c: sweep_right(kb, c, None), (jnp.zeros((1, t), F32), jnp.zeros((d, t), F32)))
        _, gq = sweep_right(qi, carry, diagonal)
        gq_ref[0] = gq

    return pl.pallas_call(
        body, name=name,
        out_shape=[jax.ShapeDtypeStruct((h, d, s), F32), jax.ShapeDtypeStruct((h, s, d), F32), jax.ShapeDtypeStruct((h, s, d), F32)],
        grid=(h, n_t),
        in_specs=[_lane_spec(d, t), _row_major_spec(t, d), _resident_spec((s, d)), _resident_spec((n_t, d, t)),
                  _resident_spec((s, d)), pl.BlockSpec((1, n_t, 1, t), lambda hh, i: (hh, 0, 0, i)),
                  _lane_spec(d, t), _row_major_spec(t, d)],
        out_specs=[_lane_spec(d, t), _resident_spec((s, d)), _resident_spec((s, d))],
        compiler_params=_cparams(("parallel", "arbitrary"), _attn_vmem(s, d, d, t, t, True)),
    )(qt, q, k, kt, v, later, dot, do)


def stick_breaking_attention(q, k, v, scale, name):
    t = min(ATTN_TILE, q.shape[1])

    def run(q, k, v):
        qb, kb, vb = (q * (scale * LOG2E)).astype(MXU_DTYPE), k.astype(MXU_DTYPE), v.astype(MXU_DTYPE)
        qt = _to_lanes(qb)
        ot, later = _stick_fwd_call(qt, kb, _key_tiles(vb, t), name)
        return _to_lanes(ot), (qt, qb, kb, vb, later)

    @jax.custom_vjp
    def op(q, k, v):
        return run(q, k, v)[0]

    def bwd(res, do):
        qt, qb, kb, vb, later = res
        gqt, gk, gv = _stick_bwd_call(qt, qb, kb, _key_tiles(kb, t), vb, later, _to_lanes(do), do.astype(MXU_DTYPE), name + "_bwd")
        return _to_lanes(gqt) * scale, gk * (1.0 / LOG2E), gv

    op.defvjp(run, bwd)
    return op(q, k, v)


def _cumsum_call(x, reverse, name):
    r, s = x.shape
    tb = min(ATTN_TILE, s)
    nb = s // tb

    def body(x_ref, o_ref, carry):
        @pl.when(pl.program_id(0) == 0)
        def _():
            carry[...] = jnp.zeros_like(carry)

        j = lax.broadcasted_iota(jnp.int32, (tb, tb), 0)
        t = lax.broadcasted_iota(jnp.int32, (tb, tb), 1)
        ones = ((j >= t) if reverse else (j <= t)).astype(MXU_DTYPE)
        rest = x_ref[...]
        y = carry[:, 0:1]
        for _ in range(3):
            term = rest.astype(MXU_DTYPE)
            rest = rest - term.astype(F32)
            y = y + _dot(term, ones)
        o_ref[...] = y
        carry[...] = jnp.broadcast_to(y[:, 0:1] if reverse else y[:, tb - 1:tb], carry.shape)

    spec = pl.BlockSpec((r, tb), (lambda i: (0, nb - 1 - i)) if reverse else (lambda i: (0, i)))
    return pl.pallas_call(
        body, name=name,
        out_shape=jax.ShapeDtypeStruct((r, s), F32),
        grid=(nb,),
        in_specs=[spec], out_specs=spec,
        scratch_shapes=[pltpu.VMEM((r, LANES), F32)],
        compiler_params=_cparams(("arbitrary",), 1024 * 1024),
    )(x)


def cumsum_lanes(x, name):
    @jax.custom_vjp
    def op(x):
        return _cumsum_call(x, False, name)

    op.defvjp(lambda x: (op(x), None), lambda _, g: (_cumsum_call(g, True, name + "_bwd"),))
    return op(x)


def _loss_call(y, target):
    n, d = y.shape
    ts = _row_tile(n, 4 * 3 * d)

    def body(y_ref, t_ref, dy_ref, loss_ref):
        err = y_ref[...] - t_ref[...]
        dy_ref[...] = err * (1.0 / d)
        part = jnp.sum(jnp.sum(err * err, axis=1, keepdims=True), axis=0, keepdims=True) * (0.5 / d)

        @pl.when(pl.program_id(0) == 0)
        def _():
            loss_ref[...] = jnp.zeros_like(loss_ref)

        loss_ref[...] += jnp.broadcast_to(part, loss_ref.shape)

    dy, loss = pl.pallas_call(
        body, name="loss_head",
        out_shape=[jax.ShapeDtypeStruct((n, d), F32), jax.ShapeDtypeStruct((SUBLANES, LANES), F32)],
        grid=(n // ts,),
        in_specs=[_row_spec(ts, d), _row_spec(ts, d)],
        out_specs=[_row_spec(ts, d), _full_spec((SUBLANES, LANES))],
        compiler_params=_cparams(("arbitrary",), 2 * ts * 4 * 3 * d),
    )(y, target)
    return loss[0, 0], dy


_FLIPS = [(0, 0, 1), (0, 1, 0), (0, 1, 1), (1, 0, 0), (1, 0, 1), (1, 1, 0), (1, 1, 1)]
MESH = pl.DeviceIdType.MESH
ANY_SPEC = pl.BlockSpec(memory_space=pl.ANY)


def _me_and_peers():
    x, y, c = lax.axis_index("x"), lax.axis_index("y"), lax.axis_index("c")
    peers = [((1 - x) if fx else x, (1 - y) if fy else y, (1 - c) if fc else c) for fx, fy, fc in _FLIPS]
    return 4 * x + 2 * y + c, peers, [4 * px + 2 * py + pc for px, py, pc in peers]


def all_gather_hbm(shard, name):
    def body(x_ref, out_ref, send_sems, recv_sems, local_sem):
        x, y, c = lax.axis_index("x"), lax.axis_index("y"), lax.axis_index("c")
        me, sibling = (x, y, c), (x, y, 1 - c)
        chips = [(1 - x, y), (x, 1 - y), (1 - x, 1 - y)]

        def block(px, py, pc):
            return out_ref.at[4 * px + 2 * py + pc]

        def copy(i, blk, to, src=None):
            return pltpu.make_async_remote_copy(src_ref=block(*blk) if src is None else src, dst_ref=block(*blk),
                                                send_sem=send_sems.at[i], recv_sem=recv_sems.at[i],
                                                device_id=to, device_id_type=MESH)

        mine = pltpu.make_async_copy(x_ref, block(*me), local_sem)
        mine.start()
        first = [copy(0, me, sibling, src=x_ref)] + [copy(1 + j, me, (*chip, c), src=x_ref) for j, chip in enumerate(chips)]
        for cp in first:
            cp.start()
        passed = [copy(4 + j, (*chip, c), sibling) for j, chip in enumerate(chips)]
        for j, chip in enumerate(chips):
            copy(1 + j, (*chip, c), me).wait_recv()
            passed[j].start()
        copy(0, sibling, me).wait_recv()
        for j, chip in enumerate(chips):
            copy(4 + j, (*chip, 1 - c), me).wait_recv()
        for cp in first + passed:
            cp.wait_send()
        mine.wait()

    return pl.pallas_call(
        body, name=name,
        out_shape=jax.ShapeDtypeStruct((N_DEV,) + shard.shape, shard.dtype),
        in_specs=[ANY_SPEC], out_specs=ANY_SPEC,
        scratch_shapes=[pltpu.SemaphoreType.DMA((N_DEV - 1,)), pltpu.SemaphoreType.DMA((N_DEV - 1,)), pltpu.SemaphoreType.DMA],
    )(shard)


def all_to_all_hbm(blocks, name):
    def body(x_ref, out_ref, send_sems, recv_sems, local_sem):
        me, peers, peer_ids = _me_and_peers()
        mine = pltpu.make_async_copy(x_ref.at[me], out_ref.at[me], local_sem)
        mine.start()
        sends = [pltpu.make_async_remote_copy(src_ref=x_ref.at[pid], dst_ref=out_ref.at[me], send_sem=send_sems.at[i],
                                              recv_sem=recv_sems.at[i], device_id=p, device_id_type=MESH)
                 for i, (p, pid) in enumerate(zip(peers, peer_ids))]
        for cp in sends:
            cp.start()
        for i, (p, pid) in enumerate(zip(peers, peer_ids)):
            pltpu.make_async_remote_copy(src_ref=x_ref.at[me], dst_ref=out_ref.at[pid], send_sem=send_sems.at[i],
                                         recv_sem=recv_sems.at[i], device_id=p, device_id_type=MESH).wait_recv()
        for cp in sends:
            cp.wait_send()
        mine.wait()

    return pl.pallas_call(
        body, name=name,
        out_shape=jax.ShapeDtypeStruct(blocks.shape, blocks.dtype),
        in_specs=[ANY_SPEC], out_specs=ANY_SPEC,
        scratch_shapes=[pltpu.SemaphoreType.DMA((N_DEV - 1,)), pltpu.SemaphoreType.DMA((N_DEV - 1,)), pltpu.SemaphoreType.DMA],
    )(blocks)


def sum_blocks(blocks, name):
    _, r, c = blocks.shape
    ts = _row_tile(r, (N_DEV * blocks.dtype.itemsize + 4) * c)

    def body(x_ref, o_ref):
        acc = x_ref[0].astype(F32)
        for d in range(1, N_DEV):
            acc = acc + x_ref[d].astype(F32)
        o_ref[...] = acc

    return pl.pallas_call(
        body, name=name,
        out_shape=jax.ShapeDtypeStruct((r, c), F32),
        grid=(r // ts,),
        in_specs=[pl.BlockSpec((N_DEV, ts, c), lambda i: (0, i, 0))],
        out_specs=_row_spec(ts, c),
        compiler_params=_cparams(("parallel",), 2 * ts * c * (N_DEV * blocks.dtype.itemsize + 4)),
    )(blocks)


def all_reduce_small(v, name):
    r, c = v.shape

    def body(x_ref, out_ref, gathered, send_sems, recv_sems):
        me, peers, peer_ids = _me_and_peers()
        sends = [pltpu.make_async_remote_copy(src_ref=x_ref, dst_ref=gathered.at[me], send_sem=send_sems.at[i],
                                              recv_sem=recv_sems.at[i], device_id=p, device_id_type=MESH)
                 for i, p in enumerate(peers)]
        for cp in sends:
            cp.start()
        gathered[me] = x_ref[...]
        for i, (p, pid) in enumerate(zip(peers, peer_ids)):
            pltpu.make_async_remote_copy(src_ref=x_ref, dst_ref=gathered.at[pid], send_sem=send_sems.at[i],
                                         recv_sem=recv_sems.at[i], device_id=p, device_id_type=MESH).wait_recv()
        for cp in sends:
            cp.wait_send()
        acc = gathered[0]
        for d in range(1, N_DEV):
            acc = acc + gathered[d]
        out_ref[...] = acc

    return pl.pallas_call(
        body, name=name,
        out_shape=jax.ShapeDtypeStruct((r, c), F32),
        in_specs=[pl.BlockSpec(memory_space=pltpu.VMEM)],
        out_specs=pl.BlockSpec(memory_space=pltpu.VMEM),
        scratch_shapes=[pltpu.VMEM((N_DEV, r, c), F32), pltpu.SemaphoreType.DMA((N_DEV - 1,)), pltpu.SemaphoreType.DMA((N_DEV - 1,))],
    )(v)


_SHARD_AXIS = {"w_in": 1, "mla_w_qb": 1, "mla_w_kvb": 2, "w_mem_kv": 1, "w_merge_up": 2, "w_branch": 3, "w_out": 1}
_SMALL = ("mla_q_norm", "mla_kv_norm", "fox_forget_bias", "ln_gain", "ln_bias")

def _regroup_qb(w):
    lead = w.shape[:-1]
    t = w.reshape(lead + (N_HEADS, MLA_NOPE + MLA_ROPE))
    half = MLA_ROPE // 2
    parts = (t[..., :MLA_NOPE], t[..., MLA_NOPE:MLA_NOPE + half], t[..., MLA_NOPE + half:])
    return jnp.concatenate([p.reshape(lead + (-1,)) for p in parts], axis=-1)


def _ungroup_qb(g):
    lead = g.shape[:-1]
    half = MLA_ROPE // 2
    n_nope, n_half = N_HEADS * MLA_NOPE, N_HEADS * half
    parts = (g[..., :n_nope].reshape(lead + (N_HEADS, MLA_NOPE)), g[..., n_nope:n_nope + n_half].reshape(lead + (N_HEADS, half)),
             g[..., n_nope + n_half:].reshape(lead + (N_HEADS, half)))
    return jnp.concatenate(parts, axis=-1).reshape(lead + (-1,))


def _to_wire(name, w):
    if name == "w_in":
        return jnp.pad(w, ((0, 0), (0, 0), (0, IN_WIDTH_PAD - IN_WIDTH)))
    if name == "mla_w_qb":
        return _regroup_qb(w)
    return w


def _from_wire(name, g):
    if name == "w_in":
        return g[:, :, :IN_WIDTH]
    if name == "mla_w_qb":
        return _ungroup_qb(g)
    return g


def _join(name, blocks):
    a = _SHARD_AXIS[name]
    t = jnp.moveaxis(blocks, 0, a)
    return t.reshape(t.shape[:a] + (t.shape[a] * t.shape[a + 1],) + t.shape[a + 2:])


def _cut(name, full):
    a = _SHARD_AXIS[name]
    t = full.reshape(full.shape[:a] + (N_DEV, full.shape[a] // N_DEV) + full.shape[a + 1:])
    return jnp.moveaxis(t, a, 0)


def _heads(t):
    s = t.shape[0]
    return t.reshape(s, N_HEADS, -1).transpose(1, 0, 2)


def _unheads(t):
    return t.transpose(1, 0, 2).reshape(t.shape[1], -1)


def _layer(x, mem, rope, w, p, tag):
    s = x.shape[0]
    cos, sin, cos4, sin4 = rope
    c_q, c_kv, k_rope, sb_qkv, fox_qkv, fox_f, mem_q, gate_z, merge_r = in_proj(x, w["w_in"], IN_SPLITS, tag + "in_proj")

    (cq_n,) = rowwise(_rms_fn, [c_q], [p["mla_q_norm"]], [MLA_Q_RANK], tag + "q_rms")
    q_all = matmul(cq_n, w["mla_w_qb"], tag + "q_up")
    n_nope, n_half = N_HEADS * MLA_NOPE, N_HEADS * MLA_ROPE // 2
    q_nope, q_x1, q_x2 = q_all[:, :n_nope], q_all[:, n_nope:n_nope + n_half], q_all[:, n_nope + n_half:]
    (ckv_n,) = rowwise(_rms_fn, [c_kv], [p["mla_kv_norm"]], [MLA_KV_RANK], tag + "kv_rms")
    kv = matmul(ckv_n, w["mla_w_kvb"], tag + "kv_up").reshape(s, N_HEADS, MLA_NOPE + HEAD_DIM)
    k_nope, v_mla = kv[..., :MLA_NOPE], kv[..., MLA_NOPE:]
    half = MLA_ROPE // 2
    q_o1, q_o2 = rowwise(_rope_fn, [q_x1, q_x2, cos4, sin4], [], [n_half, n_half], tag + "q_rope")
    k_o1, k_o2 = rowwise(_rope_fn, [k_rope[:, :half], k_rope[:, half:], cos, sin], [], [half, half], tag + "k_rope")
    zeros = jnp.zeros((s, N_HEADS, MLA_QK_PAD - MLA_NOPE - MLA_ROPE), F32)
    q_mla = jnp.concatenate([q_nope.reshape(s, N_HEADS, MLA_NOPE), q_o1.reshape(s, N_HEADS, half),
                             q_o2.reshape(s, N_HEADS, half), zeros], axis=-1).transpose(1, 0, 2)
    k_mla = jnp.concatenate([k_nope, jnp.broadcast_to(k_o1[:, None, :], (s, N_HEADS, half)),
                             jnp.broadcast_to(k_o2[:, None, :], (s, N_HEADS, half)), zeros], axis=-1).transpose(1, 0, 2)
    y_mla = softmax_attention(q_mla, k_mla, v_mla.transpose(1, 0, 2), None,
                              (MLA_NOPE + MLA_ROPE) ** -0.5, True, tag + "mla_attn")

    width = N_HEADS * HEAD_DIM
    sq, sk, sv = (_heads(sb_qkv[:, i * width:(i + 1) * width]) for i in range(3))
    y_sb = stick_breaking_attention(sq, sk, sv, HEAD_DIM ** -0.5, tag + "stick_attn")

    fq, fk, fv = (_heads(fox_qkv[:, i * width:(i + 1) * width]) for i in range(3))
    (log_f,) = rowwise(_forget_fn, [fox_f], [p["fox_forget_bias"]], [N_HEADS], tag + "forget_gate")
    log_f8 = jnp.pad(log_f.T, ((0, SUBLANES - N_HEADS), (0, 0)))
    c = cumsum_lanes(log_f8, tag + "forget_cumsum")[:N_HEADS]
    y_fox = softmax_attention(fq, fk, fv, c, HEAD_DIM ** -0.5, True, tag + "fox_attn")

    mkv = matmul(mem, w["w_mem_kv"], tag + "mem_kv")
    y_mem = softmax_attention(_heads(mem_q), _heads(mkv[:, :width]), _heads(mkv[:, width:]), None,
                              HEAD_DIM ** -0.5, False, tag + "mem_attn")

    ys = [_unheads(t) for t in (y_mla, y_sb, y_fox, y_mem)]
    yb = rowwise(_silu_gate_fn, ys + [gate_z], [], [BRANCH_WIDTH] * N_BRANCHES, tag + "silu_gate")
    proj = [matmul(yb[n], w["w_branch"][n], tag + "branch%d" % n) for n in range(N_BRANCHES)]
    merged = merge_gate(merge_r, w["w_merge_up"], proj, tag + "merge")
    out = matmul(merged, w["w_out"], tag + "out_proj")
    (xn,) = rowwise(_deepnorm_fn, [x, out], [p["ln_gain"], p["ln_bias"]], [x.shape[1]], tag + "deepnorm")
    return xn


def _adamw(g, w, m, v, name):
    shape = g.shape
    two_d = (-1, shape[-1])
    g2, w2, m2, v2 = (t.reshape(two_d) for t in (g, w, m, v))
    delta, new_m, new_v = _rowwise_fwd_call(_adamw_fn, [g2, w2, m2, v2], [], [shape[-1]] * 3, name)
    return delta.reshape(shape), new_m.reshape(shape), new_v.reshape(shape)


def kernel(x, mem, positions, w_in, mla_q_norm, mla_w_qb, mla_kv_norm, mla_w_kvb, fox_forget_bias, w_mem_kv, w_merge_up, w_branch, w_out, ln_gain, ln_bias, loss_target, m_w_in, m_mla_q_norm, m_mla_w_qb, m_mla_kv_norm, m_mla_w_kvb, m_fox_forget_bias, m_w_mem_kv, m_w_merge_up, m_w_branch, m_w_out, m_ln_gain, m_ln_bias, v_w_in, v_mla_q_norm, v_mla_w_qb, v_mla_kv_norm, v_mla_w_kvb, v_fox_forget_bias, v_w_mem_kv, v_w_merge_up, v_w_branch, v_w_out, v_ln_gain, v_ln_bias):
    weights = dict(w_in=w_in, mla_q_norm=mla_q_norm, mla_w_qb=mla_w_qb, mla_kv_norm=mla_kv_norm, mla_w_kvb=mla_w_kvb,
                   fox_forget_bias=fox_forget_bias, w_mem_kv=w_mem_kv, w_merge_up=w_merge_up, w_branch=w_branch,
                   w_out=w_out, ln_gain=ln_gain, ln_bias=ln_bias)
    m_in = dict(w_in=m_w_in, mla_q_norm=m_mla_q_norm, mla_w_qb=m_mla_w_qb, mla_kv_norm=m_mla_kv_norm, mla_w_kvb=m_mla_w_kvb,
                fox_forget_bias=m_fox_forget_bias, w_mem_kv=m_w_mem_kv, w_merge_up=m_w_merge_up, w_branch=m_w_branch,
                w_out=m_w_out, ln_gain=m_ln_gain, ln_bias=m_ln_bias)
    v_in = dict(w_in=v_w_in, mla_q_norm=v_mla_q_norm, mla_w_qb=v_mla_w_qb, mla_kv_norm=v_mla_kv_norm, mla_w_kvb=v_mla_w_kvb,
                fox_forget_bias=v_fox_forget_bias, w_mem_kv=v_w_mem_kv, w_merge_up=v_w_merge_up, w_branch=v_w_branch,
                w_out=v_w_out, ln_gain=v_ln_gain, ln_bias=v_ln_bias)
    order = list(weights)
    big = list(_SHARD_AXIS)
    n_layers = w_in.shape[0]
    x0, mem0 = x[0], mem[0]
    s = x0.shape[0]

    wire = [_to_wire(n, weights[n]).astype(WIRE_DTYPE) for n in big]
    sizes = [t.size for t in wire]
    flat = jnp.concatenate([t.reshape(-1) for t in wire]).reshape(-1, LANES)
    gathered = all_gather_hbm(flat, "gather_weights").reshape(N_DEV, -1)
    offs = np.cumsum([0] + sizes)
    full = {n: _join(n, gathered[:, offs[i]:offs[i + 1]].reshape((N_DEV,) + wire[i].shape)) for i, n in enumerate(big)}
    per_layer_w = [{n: full[n][l] for n in big} for l in range(n_layers)]
    per_layer_p = [{n: weights[n][l][None, :] for n in _SMALL} for l in range(n_layers)]

    inv_freq = ROPE_THETA ** (-jnp.arange(0, MLA_ROPE, 2, dtype=F32) / MLA_ROPE)
    ang = positions[0].astype(F32)[:, None] * inv_freq
    cos, sin = jnp.cos(ang), jnp.sin(ang)
    rope = (cos, sin, jnp.tile(cos, (1, N_HEADS)), jnp.tile(sin, (1, N_HEADS)))

    def trunk(x, ws, ps):
        for l in range(n_layers):
            x = _layer(x, mem0, rope, ws[l], ps[l], "l%d_" % l)
        return x

    y, pullback = jax.vjp(trunk, x0, per_layer_w, per_layer_p)
    loss_part, dy = _loss_call(y, loss_target[0])
    grad_x, grad_ws, grad_ps = pullback(dy)

    cut = [_cut(n, jnp.stack([grad_ws[l][n] for l in range(n_layers)])) for n in big]
    blocks = jnp.concatenate([t.reshape(N_DEV, -1) for t in cut], axis=1).reshape(N_DEV, -1, LANES)
    received = all_to_all_hbm(blocks, "scatter_grads")
    summed = sum_blocks(received, "sum_grads").reshape(-1)
    grads = {n: _from_wire(n, summed[offs[i]:offs[i + 1]].reshape(wire[i].shape)) for i, n in enumerate(big)}

    small = [jnp.stack([grad_ps[l][n][0] for l in range(n_layers)]) for n in _SMALL]
    small_sizes = [t.size for t in small]
    packed = jnp.concatenate([t.reshape(-1) for t in small] + [loss_part.reshape(1)])
    rows = -(-packed.size // (SUBLANES * LANES)) * SUBLANES
    packed = jnp.pad(packed, (0, rows * LANES - packed.size)).reshape(rows, LANES)
    reduced = all_reduce_small(packed, "reduce_small").reshape(-1)
    small_offs = np.cumsum([0] + small_sizes)
    for i, n in enumerate(_SMALL):
        grads[n] = reduced[small_offs[i]:small_offs[i + 1]].reshape(small[i].shape)
    loss = reduced[small_offs[-1]]

    delta, new_m, new_v = {}, {}, {}
    for n in order:
        delta[n], new_m[n], new_v[n] = _adamw(grads[n], weights[n], m_in[n], v_in[n], "adamw_" + n)
    return (loss, grad_x[None], *[grads[n] for n in order], *[delta[n] for n in order],
            *[new_m[n] for n in order], *[new_v[n] for n in order])
```
